```python
import math
import jax, jax.numpy as jnp
from jax import lax
import numpy as np

D_MODEL = 1024
BATCH = 8
SEQ = 2048
DEPTH = 4
DEC_BATCH = 128
DEC_SEQ = 1
PAST_LEN = 16384
PAGE_SIZE = 128

N_MIXERS = 2
N_A_LAYERS = (DEPTH + 1) // 2
N_B_LAYERS = DEPTH // 2
DK_A = 128
H_A = D_MODEL // DK_A
DV_A = D_MODEL // H_A
CHUNK_A = 32
CHUNK_B = 128
D_INNER_B = D_MODEL
G_B = 8
DG_B = D_INNER_B // G_B
D_FF = -(-(8 * D_MODEL) // (3 * 256)) * 256
ALPHA = (2 * DEPTH) ** 0.25
BETA = (8 * DEPTH) ** -0.25
LN_EPS = 1e-5
RMS_EPS = 1e-6

kernel_name = "hgrn2_chunkmlp_hybrid_step"


def layer_norm(x, g, b):
    xf = x.astype(jnp.float32)
    mu = jnp.mean(xf, axis=-1, keepdims=True)
    var = jnp.mean(jnp.square(xf - mu), axis=-1, keepdims=True)
    return ((xf - mu) * lax.rsqrt(var + LN_EPS) * g + b).astype(x.dtype)


def rms_norm(x, g):
    xf = x.astype(jnp.float32)
    return xf * lax.rsqrt(jnp.mean(jnp.square(xf), axis=-1, keepdims=True) + RMS_EPS) * g


def hgrn2_recurrence(q, k, v, logf, s0):
    b, L, h, dk = q.shape
    dv = v.shape[-1]
    c = math.gcd(L, CHUNK_A)
    n = L // c

    def to_chunks(a):
        return a.reshape(b, n, c, h, a.shape[-1]).transpose(1, 0, 3, 2, 4)

    causal = jnp.tril(jnp.ones((c, c), dtype=bool))[:, :, None]

    def step(s, blk):
        qb, kb, vb, gb = blk
        g = jnp.cumsum(gb, axis=2)
        o_inter = jnp.einsum("bhtk,bhkv->bhtv", qb * jnp.exp(g), s)
        diff = g[:, :, :, None, :] - g[:, :, None, :, :]
        decay = jnp.exp(jnp.where(causal, diff, -jnp.inf))
        a = jnp.einsum("bhtk,bhsk,bhtsk->bhts", qb, kb, decay)
        o_intra = jnp.einsum("bhts,bhsv->bhtv", a, vb)
        g_end = g[:, :, -1:, :]
        s_new = jnp.exp(g_end[:, :, 0, :, None]) * s + jnp.einsum(
            "bhsk,bhsv->bhkv", kb * jnp.exp(g_end - g), vb)
        return s_new, o_inter + o_intra

    s_fin, o = lax.scan(step, s0, (to_chunks(q), to_chunks(k), to_chunks(v), to_chunks(logf)))
    o = o.transpose(1, 0, 3, 2, 4).reshape(b, L, h, dv)
    return o, s_fin


def hgrn2_mixer(x, lower_bound, w_in, norm_g, w_out, s0):
    b, L, _ = x.shape
    dq = H_A * DK_A
    proj = (x @ w_in).astype(jnp.float32)
    q_raw, f_raw, i_raw, g_raw = jnp.split(proj, [dq, 2 * dq, 2 * dq + H_A * DV_A], axis=-1)
    q = jax.nn.silu(q_raw).reshape(b, L, H_A, DK_A) * (DK_A ** -0.5)
    f = lower_bound + (1.0 - lower_bound) * jax.nn.sigmoid(f_raw)
    k = (1.0 - f).reshape(b, L, H_A, DK_A)
    logf = jnp.log(f).reshape(b, L, H_A, DK_A)
    v = i_raw.reshape(b, L, H_A, DV_A)
    o, s_fin = hgrn2_recurrence(q, k, v, logf, s0.astype(jnp.float32))
    o = rms_norm(o, norm_g) * jax.nn.silu(g_raw).reshape(b, L, H_A, DV_A)
    y = o.reshape(b, L, H_A * DV_A).astype(x.dtype) @ w_out
    return y, s_fin


def chunk_mlp(x, w_in, ln_g, ln_b, w_s, b_s, w_out):
    b, L, _ = x.shape
    hdn = jax.nn.gelu(x @ w_in, approximate=False)
    u, v = jnp.split(hdn, 2, axis=-1)
    v = layer_norm(v, ln_g, ln_b)
    c = min(L, CHUNK_B)
    n = L // c
    w = jnp.tril(w_s[:, :c, :c])
    bias = b_s[:, :c].T[None, None, :, :, None]
    mixed = jnp.einsum("gts,bnsgd->bntgd", w, v.reshape(b, n, c, G_B, DG_B)) + bias
    y = (u * mixed.reshape(b, L, D_INNER_B)).astype(x.dtype) @ w_out
    return y, v[:, L - c:]


def swiglu_ffn(x, w_in, w_out):
    gate, up = jnp.split(x @ w_in, 2, axis=-1)
    return (jax.nn.silu(gate) * up) @ w_out


def setup_inputs(seed: int = 0) -> dict:
    key = jax.random.key(seed)
    ks = jax.random.split(key, 19)
    f32 = jnp.float32

    def nrm(k, shape, scale):
        return jax.random.normal(k, shape, f32) * scale

    return {
        "x_prompt": nrm(ks[0], (BATCH, SEQ, D_MODEL), 1.0),
        "x_sample": nrm(ks[1], (DEC_BATCH, DEC_SEQ, D_MODEL), 1.0),
        "state_hgrn": nrm(ks[2], (N_A_LAYERS, DEC_BATCH, H_A, DK_A, DV_A), 0.5),
        "ln_mix_g": 1.0 + nrm(ks[3], (DEPTH, D_MODEL), 0.05),
        "ln_mix_b": nrm(ks[4], (DEPTH, D_MODEL), 0.05),
        "ln_ffn_g": 1.0 + nrm(ks[5], (DEPTH, D_MODEL), 0.05),
        "ln_ffn_b": nrm(ks[6], (DEPTH, D_MODEL), 0.05),
        "a_lb_raw": nrm(ks[7], (DEPTH, H_A * DK_A), 0.5),
        "a_w_in": nrm(ks[8], (N_A_LAYERS, D_MODEL, 2 * H_A * DK_A + 2 * H_A * DV_A), D_MODEL ** -0.5),
        "a_norm_g": 1.0 + nrm(ks[9], (N_A_LAYERS, DV_A), 0.05),
        "a_w_out": nrm(ks[10], (N_A_LAYERS, H_A * DV_A, D_MODEL), BETA * (H_A * DV_A) ** -0.5),
        "b_w_in": nrm(ks[11], (N_B_LAYERS, D_MODEL, 2 * D_INNER_B), D_MODEL ** -0.5),
        "b_ln_g": 1.0 + nrm(ks[12], (N_B_LAYERS, D_INNER_B), 0.05),
        "b_ln_b": nrm(ks[13], (N_B_LAYERS, D_INNER_B), 0.05),
        "b_w_s": nrm(ks[14], (N_B_LAYERS, G_B, CHUNK_B, CHUNK_B), CHUNK_B ** -0.5),
        "b_bias_s": 1.0 + nrm(ks[15], (N_B_LAYERS, G_B, CHUNK_B), 0.1),
        "b_w_out": nrm(ks[16], (N_B_LAYERS, D_INNER_B, D_MODEL), BETA * D_INNER_B ** -0.5),
        "ffn_w_in": nrm(ks[17], (DEPTH, D_MODEL, 2 * D_FF), D_MODEL ** -0.5),
        "ffn_w_out": nrm(ks[18], (DEPTH, D_FF, D_MODEL), BETA * D_FF ** -0.5),
    }


def reference(x_prompt, x_sample, state_hgrn, ln_mix_g, ln_mix_b, ln_ffn_g, ln_ffn_b,
              a_lb_raw, a_w_in, a_norm_g, a_w_out, b_w_in, b_ln_g, b_ln_b, b_w_s, b_bias_s,
              b_w_out, ffn_w_in, ffn_w_out):
    p = jax.nn.softmax(a_lb_raw.astype(jnp.float32), axis=0)
    lower_bounds = jnp.cumsum(p, axis=0) - p[0]

    def trunk(x, hgrn_init):
        hgrn_out, v_out = [], []
        for layer in range(DEPTH):
            j = layer // N_MIXERS
            if layer % N_MIXERS == 0:
                h, s = hgrn2_mixer(x, lower_bounds[layer], a_w_in[j], a_norm_g[j], a_w_out[j], hgrn_init[j])
                hgrn_out.append(s.astype(state_hgrn.dtype))
            else:
                h, v_rows = chunk_mlp(x, b_w_in[j], b_ln_g[j], b_ln_b[j], b_w_s[j], b_bias_s[j], b_w_out[j])
                v_out.append(v_rows)
            x = layer_norm(ALPHA * x + h, ln_mix_g[layer], ln_mix_b[layer])
            x = layer_norm(ALPHA * x + swiglu_ffn(x, ffn_w_in[layer], ffn_w_out[layer]),
                           ln_ffn_g[layer], ln_ffn_b[layer])
        return x, jnp.stack(hgrn_out), jnp.stack(v_out)

    prompt_init = jnp.zeros((N_A_LAYERS, x_prompt.shape[0], H_A, DK_A, DV_A), jnp.float32)
    y_prompt, hgrn_state_prompt, chunk_v_prompt = trunk(x_prompt, prompt_init)
    y_sample, hgrn_state_sample, chunk_v_sample = trunk(x_sample, state_hgrn)
    return (y_prompt, y_sample, hgrn_state_prompt, hgrn_state_sample, chunk_v_prompt, chunk_v_sample)
```

```python
import functools

import numpy as np
import jax
import jax.numpy as jnp
from jax import lax
from jax.experimental import pallas as pl
from jax.experimental.pallas import tpu as pltpu

D_MODEL = 1024
DEPTH = 4
N_MIXERS = 2
DK_A = 128
H_A = D_MODEL // DK_A
DV_A = D_MODEL // H_A
D_INNER_B = D_MODEL
CHUNK_B = 128
G_B = 8
DG_B = D_INNER_B // G_B
D_FF = -(-(8 * D_MODEL) // (3 * 256)) * 256
ALPHA = (2 * DEPTH) ** 0.25
LN_EPS = 1e-5
RMS_EPS = 1e-6

LANES = 128
SUBLANES = 8
MXU_DIM = 256
VMEM_LIMIT_BYTES = 56 * 2**20

ROW_TILE = 512
CHUNK_A = 128
FF_TILE = MXU_DIM
DEC_SEQS = 8

BF16 = jnp.bfloat16
F32 = jnp.float32


def _level_halves(chunk):
    out, h = [], chunk // 2
    while h >= 1:
        out.append(h)
        h //= 2
    return tuple(out)


LEVEL_HALVES = _level_halves(CHUNK_A)


def _level_masks(chunk):
    t = np.arange(chunk)[:, None]
    s = np.arange(chunk)[None, :]
    masks = [((t // (2 * h) == s // (2 * h)) & (t % (2 * h) >= h) & (s % (2 * h) < h))
             for h in _level_halves(chunk)]
    masks.append(t == s)
    return np.stack(masks).astype(np.float32)


def _dot(a, b):
    return jnp.dot(a, b, preferred_element_type=F32)


def _dot_nt(a, b):
    return lax.dot_general(a, b, (((1,), (1,)), ((), ())), preferred_element_type=F32)


def _dot_tn(a, b):
    return lax.dot_general(a, b, (((0,), (0,)), ((), ())), preferred_element_type=F32)


def _layer_norm(z, g, b):
    mu = jnp.mean(z, axis=-1, keepdims=True)
    zc = z - mu
    var = jnp.mean(zc * zc, axis=-1, keepdims=True)
    return zc * lax.rsqrt(var + LN_EPS) * g + b


def _gelu(x):
    return 0.5 * x * (1.0 + lax.erf(x * (2.0 ** -0.5)))


def _lower_bound(raw, layer):
    m = jnp.max(raw, axis=0, keepdims=True)
    e = jnp.exp(raw - m)
    p = e / jnp.sum(e, axis=0, keepdims=True)
    c = p[0:1]
    for i in range(1, layer + 1):
        c = c + p[i:i + 1]
    return c - p[0:1]


def _hgrn_gates(xb, win_ref, lb):
    d = D_MODEL
    q = jax.nn.silu(_dot(xb, win_ref[:, 0:d])) * (DK_A ** -0.5)
    f = lb + (1.0 - lb) * jax.nn.sigmoid(_dot(xb, win_ref[:, d:2 * d]))
    v = _dot(xb, win_ref[:, 2 * d:3 * d])
    gate = jax.nn.silu(_dot(xb, win_ref[:, 3 * d:4 * d]))
    return q, f, v, gate


def _cumsum_rows(a):
    n, w = a.shape
    row = lax.broadcasted_iota(jnp.int32, a.shape, 0)
    s = 1
    while s < n:
        if s < SUBLANES:
            a = a + jnp.where(row >= s, pltpu.roll(a, s, axis=0), 0.0)
        else:
            a = a + jnp.concatenate([jnp.zeros((s, w), a.dtype), a[:n - s]], axis=0)
        s *= 2
    return a


def _level_ref(g, half):
    n, w = g.shape
    if half >= SUBLANES // 2:
        blk = 2 * half
        parts = [jnp.broadcast_to(g[b * blk + half - 1:b * blk + half, :], (blk, w))
                 for b in range(n // blk)]
        return parts[0] if len(parts) == 1 else jnp.concatenate(parts, axis=0)
    row = lax.broadcasted_iota(jnp.int32, g.shape, 0)
    if half == 2:
        m = row % 4
        return jnp.where(m == 0, pltpu.roll(g, n - 1, axis=0),
                         jnp.where(m == 1, g,
                                   jnp.where(m == 2, pltpu.roll(g, 1, axis=0),
                                             pltpu.roll(g, 2, axis=0))))
    assert half == 1
    return jnp.where(row % 2 == 0, g, pltpu.roll(g, 1, axis=0))


def _hgrn_head_chunk(h, c, q_s, k_s, v_s, lf_s, oh_s, st_ref, mask_ref, norm_g):
    rows = pl.ds(pl.multiple_of(c * CHUNK_A, CHUNK_A), CHUNK_A)
    q = q_s[h, rows, :]
    k = k_s[h, rows, :]
    vb = v_s[h, rows, :].astype(BF16)
    g = _cumsum_rows(lf_s[h, rows, :])
    g_last = g[CHUNK_A - 1:CHUNK_A, :]
    st = st_ref[h]
    o = _dot_nt((q * jnp.exp(g)).astype(BF16), st.astype(BF16))
    ks = (k * jnp.exp(g_last - g)).astype(BF16)
    st_ref[h] = st * jnp.exp(g_last) + _dot_tn(vb, ks)
    a = mask_ref[len(LEVEL_HALVES)] * _dot_nt(q.astype(BF16), k.astype(BF16))
    for lvl, half in enumerate(LEVEL_HALVES):
        e = jnp.exp(-jnp.abs(g - _level_ref(g, half)))
        a = a + mask_ref[lvl] * _dot_nt((q * e).astype(BF16), (k * e).astype(BF16))
    o = o + _dot(a.astype(BF16), vb)
    ms = jnp.mean(o * o, axis=-1, keepdims=True)
    oh_s[h, rows, :] = o * lax.rsqrt(ms + RMS_EPS) * norm_g


def _hgrn_prompt_kernel(layer, x_ref, lbraw_ref, win_ref, ng_ref, wout_ref, lng_ref, lnb_ref,
                        mask_ref, o_ref, sfin_ref, st_ref, q_s, k_s, v_s, lf_s, oh_s, gate_s):
    t = pl.program_id(1)

    @pl.when(t == 0)
    def _():
        st_ref[...] = jnp.zeros_like(st_ref)

    x = x_ref[0]
    q, f, v, gate = _hgrn_gates(x.astype(BF16), win_ref, _lower_bound(lbraw_ref[...], layer))
    k = 1.0 - f
    lf = jnp.log(f)
    gate_s[...] = gate
    for h in range(H_A):
        ls = slice(h * DK_A, (h + 1) * DK_A)
        q_s[h] = q[:, ls]
        k_s[h] = k[:, ls]
        v_s[h] = v[:, ls]
        lf_s[h] = lf[:, ls]

    norm_g = ng_ref[...]

    def head_body(h, carry):
        def chunk_body(c, carry2):
            _hgrn_head_chunk(h, c, q_s, k_s, v_s, lf_s, oh_s, st_ref, mask_ref, norm_g)
            return carry2
        return lax.fori_loop(0, ROW_TILE // CHUNK_A, chunk_body, carry)

    lax.fori_loop(0, H_A, head_body, 0)

    o = jnp.concatenate([oh_s[h] for h in range(H_A)], axis=1) * gate_s[...]
    y = _dot(o.astype(BF16), wout_ref[...])
    o_ref[0] = _layer_norm(ALPHA * x + y, lng_ref[...], lnb_ref[...])

    @pl.when(t == pl.num_programs(1) - 1)
    def _():
        for h in range(H_A):
            sfin_ref[0, h] = st_ref[h].T


def _const_spec(shape):
    nd = len(shape)
    return pl.BlockSpec(shape, lambda *_: (0,) * nd, pipeline_mode=pl.Buffered(1))


def _params(n_axes):
    return pltpu.CompilerParams(dimension_semantics=("arbitrary",) * n_axes,
                                vmem_limit_bytes=VMEM_LIMIT_BYTES)


def _hgrn_prompt(x, layer, lb_raw, w_in, norm_g, w_out, ln_g, ln_b, masks):
    b, l, d = x.shape
    assert l % ROW_TILE == 0 and ROW_TILE % CHUNK_A == 0
    head_scratch = pltpu.VMEM((H_A, ROW_TILE, DK_A), F32)
    return pl.pallas_call(
        functools.partial(_hgrn_prompt_kernel, layer),
        grid=(b, l // ROW_TILE),
        in_specs=[
            pl.BlockSpec((1, ROW_TILE, d), lambda i, t: (i, t, 0)),
            _const_spec(lb_raw.shape),
            _const_spec(w_in.shape),
            _const_spec(norm_g.shape),
            _const_spec(w_out.shape),
            _const_spec(ln_g.shape),
            _const_spec(ln_b.shape),
            _const_spec(masks.shape),
        ],
        out_specs=[
            pl.BlockSpec((1, ROW_TILE, d), lambda i, t: (i, t, 0)),
            pl.BlockSpec((1, H_A, DK_A, DV_A), lambda i, t: (i, 0, 0, 0)),
        ],
        out_shape=[
            jax.ShapeDtypeStruct((b, l, d), F32),
            jax.ShapeDtypeStruct((b, H_A, DK_A, DV_A), F32),
        ],
        scratch_shapes=[
            pltpu.VMEM((H_A, DV_A, DK_A), F32),
            head_scratch, head_scratch, head_scratch, head_scratch, head_scratch,
            pltpu.VMEM((ROW_TILE, d), F32),
        ],
        compiler_params=_params(2),
        name="hgrn_prompt",
    )(x, lb_raw, w_in, norm_g, w_out, ln_g, ln_b, masks)


def _split3(f):
    hi = f.astype(BF16).astype(F32)
    r = f - hi
    mid = r.astype(BF16).astype(F32)
    lo = (r - mid).astype(BF16).astype(F32)
    return hi, mid, lo


def _hgrn_decode_kernel(layer, x_ref, lbraw_ref, win_ref, ng_ref, wout_ref, lng_ref, lnb_ref,
                        s_ref, o_ref, snew_ref, f3_s, k_s, q_s, v_s, gate_s, oacc_s):
    step = pl.program_id(0)
    d = D_MODEL
    nb = DEC_SEQS

    @pl.when(step == 0)
    def _():
        q, f, v, gate = _hgrn_gates(x_ref[...].astype(BF16), win_ref,
                                    _lower_bound(lbraw_ref[...], layer))
        hi, mid, lo = _split3(f)
        f3_s[0] = hi
        f3_s[1] = mid
        f3_s[2] = lo
        k_s[...] = 1.0 - f
        q_s[...] = q
        v_s[...] = v
        gate_s[...] = gate

    rows = pl.ds(pl.multiple_of(step * nb, nb), nb)
    fhi = f3_s[0, rows, :]
    fmid = f3_s[1, rows, :]
    flo = f3_s[2, rows, :]
    k8 = k_s[rows, :]
    q8 = q_s[rows, :]
    v8 = v_s[rows, :]
    ones = jnp.ones((nb, DV_A), F32)
    zeros = jnp.zeros((nb, DV_A), F32)
    rhs = []
    for h in range(H_A):
        ls = slice(h * DK_A, (h + 1) * DK_A)
        left = jnp.concatenate([ones, ones, ones, zeros], axis=0)
        right = jnp.concatenate([zeros, zeros, zeros, v8[:, ls]], axis=0)
        rhs.append(jnp.concatenate([left, right], axis=1).astype(BF16))
    seq = lax.broadcasted_iota(jnp.int32, (nb, DK_A), 0)

    def seq_body(i, o8):
        sel = seq == i
        outs = []
        for h in range(H_A):
            ls = slice(h * DK_A, (h + 1) * DK_A)
            lhs = jnp.concatenate([jnp.where(sel, a[:, ls], 0.0) for a in (fhi, fmid, flo, k8)],
                                  axis=0).astype(BF16)
            fb_kv = _dot_tn(lhs, rhs[h])
            s_new = fb_kv[:, :DV_A] * s_ref[i, h] + fb_kv[:, DV_A:]
            snew_ref[i, h] = s_new
            outs.append(_dot(jnp.where(sel, q8[:, ls], 0.0).astype(BF16), s_new.astype(BF16)))
        return o8 + jnp.concatenate(outs, axis=1)

    oacc_s[rows, :] = lax.fori_loop(0, nb, seq_body, jnp.zeros((nb, d), F32))

    @pl.when(step == pl.num_programs(0) - 1)
    def _():
        o = oacc_s[...]
        norm_g = ng_ref[...]
        heads = []
        for h in range(H_A):
            oh = o[:, h * DV_A:(h + 1) * DV_A]
            ms = jnp.mean(oh * oh, axis=-1, keepdims=True)
            heads.append(oh * lax.rsqrt(ms + RMS_EPS) * norm_g)
        on = jnp.concatenate(heads, axis=1) * gate_s[...]
        y = _dot(on.astype(BF16), wout_ref[...])
        o_ref[...] = _layer_norm(ALPHA * x_ref[...] + y, lng_ref[...], lnb_ref[...])


def _hgrn_decode(x, layer, lb_raw, w_in, norm_g, w_out, ln_g, ln_b, states, j):
    n, d = x.shape
    assert n % DEC_SEQS == 0
    row_scratch = pltpu.VMEM((n, d), F32)
    state_block = (None, DEC_SEQS, H_A, DK_A, DV_A)
    return pl.pallas_call(
        functools.partial(_hgrn_decode_kernel, layer),
        grid=(n // DEC_SEQS,),
        in_specs=[
            _const_spec(x.shape),
            _const_spec(lb_raw.shape),
            _const_spec(w_in.shape),
            _const_spec(norm_g.shape),
            _const_spec(w_out.shape),
            _const_spec(ln_g.shape),
            _const_spec(ln_b.shape),
            pl.BlockSpec(state_block, lambda s: (j, s, 0, 0, 0)),
        ],
        out_specs=[
            pl.BlockSpec((n, d), lambda s: (0, 0)),
            pl.BlockSpec(state_block[1:], lambda s: (s, 0, 0, 0)),
        ],
        out_shape=[
            jax.ShapeDtypeStruct((n, d), F32),
            jax.ShapeDtypeStruct(states.shape[1:], F32),
        ],
        scratch_shapes=[
            pltpu.VMEM((3, n, d), F32),
            row_scratch, row_scratch, row_scratch, row_scratch, row_scratch,
        ],
        compiler_params=_params(1),
        name="hgrn_decode",
    )(x, lb_raw, w_in, norm_g, w_out, ln_g, ln_b, states)


def _cmlp_kernel(single, x_ref, win_ref, vg_ref, vb_ref, ws_ref, bias_ref, wout_ref,
                 lng_ref, lnb_ref, o_ref, vrows_ref):
    x = x_ref[...]
    n = x.shape[0]
    xb = x.astype(BF16)
    di = D_INNER_B
    u = _gelu(_dot(xb, win_ref[:, 0:di]))
    v = _gelu(_dot(xb, win_ref[:, di:2 * di]))
    v = _layer_norm(v, vg_ref[...], vb_ref[...])
    if single:
        mixed = v * ws_ref[...] + bias_ref[...]
        vrows_ref[...] = v
    else:
        row = lax.broadcasted_iota(jnp.int32, (CHUNK_B, CHUNK_B), 0)
        col = lax.broadcasted_iota(jnp.int32, (CHUNK_B, CHUNK_B), 1)
        w = [jnp.where(row >= col, ws_ref[g], 0.0).astype(BF16) for g in range(G_B)]
        bias = bias_ref[...]
        vb16 = v.astype(BF16)
        chunks = []
        for c in range(n // CHUNK_B):
            rs = slice(c * CHUNK_B, (c + 1) * CHUNK_B)
            groups = [_dot(w[g], vb16[rs, g * DG_B:(g + 1) * DG_B]) for g in range(G_B)]
            chunks.append(jnp.concatenate(groups, axis=1) + bias)
        mixed = jnp.concatenate(chunks, axis=0)

        @pl.when(pl.program_id(1) == pl.num_programs(1) - 1)
        def _():
            vrows_ref[...] = v[n - CHUNK_B:, :]

    y = _dot((u * mixed).astype(BF16), wout_ref[...])
    o_ref[...] = _layer_norm(ALPHA * x + y, lng_ref[...], lnb_ref[...])


def _cmlp_prompt(x, w_in, v_g, v_b, w_s, bias_full, w_out, ln_g, ln_b):
    b, l, d = x.shape
    assert l % ROW_TILE == 0 and ROW_TILE % CHUNK_B == 0
    return pl.pallas_call(
        functools.partial(_cmlp_kernel, False),
        grid=(b, l // ROW_TILE),
        in_specs=[
            pl.BlockSpec((None, ROW_TILE, d), lambda i, t: (i, t, 0)),
            _const_spec(w_in.shape),
            _const_spec(v_g.shape),
            _const_spec(v_b.shape),
            _const_spec(w_s.shape),
            _const_spec(bias_full.shape),
            _const_spec(w_out.shape),
            _const_spec(ln_g.shape),
            _const_spec(ln_b.shape),
        ],
        out_specs=[
            pl.BlockSpec((None, ROW_TILE, d), lambda i, t: (i, t, 0)),
            pl.BlockSpec((None, CHUNK_B, D_INNER_B), lambda i, t: (i, 0, 0)),
        ],
        out_shape=[
            jax.ShapeDtypeStruct((b, l, d), F32),
            jax.ShapeDtypeStruct((b, CHUNK_B, D_INNER_B), F32),
        ],
        compiler_params=_params(2),
        name="cmlp_prompt",
    )(x, w_in, v_g, v_b, w_s, bias_full, w_out, ln_g, ln_b)


def _cmlp_decode(x, w_in, v_g, v_b, w_row, bias_row, w_out, ln_g, ln_b):
    n, d = x.shape
    operands = (x, w_in, v_g, v_b, w_row, bias_row, w_out, ln_g, ln_b)
    return pl.pallas_call(
        functools.partial(_cmlp_kernel, True),
        grid=(1,),
        in_specs=[_const_spec(a.shape) for a in operands],
        out_specs=[
            pl.BlockSpec((n, d), lambda s: (0, 0)),
            pl.BlockSpec((n, D_INNER_B), lambda s: (0, 0)),
        ],
        out_shape=[
            jax.ShapeDtypeStruct((n, d), F32),
            jax.ShapeDtypeStruct((n, D_INNER_B), F32),
        ],
        compiler_params=_params(1),
        name="cmlp_decode",
    )(*operands)


def _ffn_kernel(x_ref, win_ref, wout_ref, lng_ref, lnb_ref, o_ref):
    x = x_ref[...]
    xb = x.astype(BF16)
    acc = jnp.zeros(x.shape, F32)
    for j in range(D_FF // FF_TILE):
        cs = slice(j * FF_TILE, (j + 1) * FF_TILE)
        us = slice(D_FF + j * FF_TILE, D_FF + (j + 1) * FF_TILE)
        hcol = jax.nn.silu(_dot(xb, win_ref[:, cs])) * _dot(xb, win_ref[:, us])
        acc = acc + _dot(hcol.astype(BF16), wout_ref[cs, :])
    o_ref[...] = _layer_norm(ALPHA * x + acc, lng_ref[...], lnb_ref[...])


def _ffn(x, w_in, w_out, ln_g, ln_b, row_tile):
    n, d = x.shape
    assert n % row_tile == 0 and D_FF % FF_TILE == 0
    return pl.pallas_call(
        _ffn_kernel,
        grid=(n // row_tile,),
        in_specs=[
            pl.BlockSpec((row_tile, d), lambda i: (i, 0)),
            _const_spec(w_in.shape),
            _const_spec(w_out.shape),
            _const_spec(ln_g.shape),
            _const_spec(ln_b.shape),
        ],
        out_specs=pl.BlockSpec((row_tile, d), lambda i: (i, 0)),
        out_shape=jax.ShapeDtypeStruct((n, d), F32),
        compiler_params=_params(1),
        name="ffn",
    )(x, w_in, w_out, ln_g, ln_b)


def kernel(x_prompt, x_sample, state_hgrn, ln_mix_g, ln_mix_b, ln_ffn_g, ln_ffn_b, a_lb_raw, a_w_in, a_norm_g, a_w_out, b_w_in, b_ln_g, b_ln_b, b_w_s, b_bias_s, b_w_out, ffn_w_in, ffn_w_out):
    bsz, seq, d = x_prompt.shape
    n_dec = x_sample.shape[0]
    assert x_sample.shape[1] == 1 and seq % CHUNK_B == 0

    a_w_in_b = a_w_in.astype(BF16)
    a_w_out_b = a_w_out.astype(BF16)
    b_w_in_b = b_w_in.astype(BF16)
    b_w_out_b = b_w_out.astype(BF16)
    ffn_w_in_b = ffn_w_in.astype(BF16)
    ffn_w_out_b = ffn_w_out.astype(BF16)
    masks = jnp.asarray(_level_masks(CHUNK_A))
    lb_raw = a_lb_raw.astype(F32)
    bias_full = jnp.repeat(jnp.swapaxes(b_bias_s, 1, 2), DG_B, axis=2)
    w_row = jnp.repeat(b_w_s[:, :, 0, 0], DG_B, axis=1)[:, None, :]

    def row(p, i):
        return p[i][None, :]

    xp = x_prompt
    xs = x_sample.reshape(n_dec, d)
    st_p, st_s, v_p, v_s = [], [], [], []
    for layer in range(DEPTH):
        j = layer // N_MIXERS
        mg, mb = row(ln_mix_g, layer), row(ln_mix_b, layer)
        if layer % N_MIXERS == 0:
            ng = a_norm_g[j][None, :]
            xp, s = _hgrn_prompt(xp, layer, lb_raw, a_w_in_b[j], ng, a_w_out_b[j], mg, mb, masks)
            st_p.append(s)
            xs, s = _hgrn_decode(xs, layer, lb_raw, a_w_in_b[j], ng, a_w_out_b[j], mg, mb,
                                 state_hgrn, j)
            st_s.append(s)
        else:
            vg, vb = row(b_ln_g, j), row(b_ln_b, j)
            xp, v = _cmlp_prompt(xp, b_w_in_b[j], vg, vb, b_w_s[j], bias_full[j], b_w_out_b[j],
                                 mg, mb)
            v_p.append(v)
            xs, v = _cmlp_decode(xs, b_w_in_b[j], vg, vb, w_row[j], bias_full[j][0:1],
                                 b_w_out_b[j], mg, mb)
            v_s.append(v[:, None, :])
        fg, fb = row(ln_ffn_g, layer), row(ln_ffn_b, layer)
        xp = _ffn(xp.reshape(bsz * seq, d), ffn_w_in_b[layer], ffn_w_out_b[layer], fg, fb,
                  ROW_TILE).reshape(bsz, seq, d)
        xs = _ffn(xs, ffn_w_in_b[layer], ffn_w_out_b[layer], fg, fb, n_dec)

    return (xp, xs.reshape(n_dec, 1, d), jnp.stack(st_p), jnp.stack(st_s),
            jnp.stack(v_p), jnp.stack(v_s))
```

```python
import functools

import numpy as np
import jax
import jax.numpy as jnp
from jax import lax
from jax.experimental import pallas as pl
from jax.experimental.pallas import tpu as pltpu

D_MODEL = 1024
DEPTH = 4
N_MIXERS = 2
DK_A = 128
H_A = D_MODEL // DK_A
DV_A = D_MODEL // H_A
D_INNER_B = D_MODEL
CHUNK_B = 128
G_B = 8
DG_B = D_INNER_B // G_B
D_FF = -(-(8 * D_MODEL) // (3 * 256)) * 256
ALPHA = (2 * DEPTH) ** 0.25
LN_EPS = 1e-5
RMS_EPS = 1e-6

LANES = 128
SUBLANES = 8
MXU_DIM = 256
VMEM_LIMIT_BYTES = 56 * 2**20

ROW_TILE = 512
CHUNK_A = 128
HEAD_GROUP = 8
FF_TILE = MXU_DIM
DEC_SEQS = 8

BF16 = jnp.bfloat16
F32 = jnp.float32


def _level_halves(chunk):
    out, h = [], chunk // 2
    while h >= 1:
        out.append(h)
        h //= 2
    return tuple(out)


LEVEL_HALVES = _level_halves(CHUNK_A)


def _level_masks(chunk):
    t = np.arange(chunk)[:, None]
    s = np.arange(chunk)[None, :]
    masks = [((t // (2 * h) == s // (2 * h)) & (t % (2 * h) >= h) & (s % (2 * h) < h))
             for h in _level_halves(chunk)]
    masks.append(t == s)
    return np.stack(masks).astype(np.float32)


def _level_signs(chunk, width):
    t = np.arange(chunk)[:, None]
    signs = [np.where(t % (2 * h) >= h, 1.0, -1.0) for h in _level_halves(chunk)]
    return np.broadcast_to(np.stack(signs), (len(signs), chunk, width)).astype(np.float32)


def _dot(a, b):
    return jnp.dot(a, b, preferred_element_type=F32)


def _dot_nt(a, b):
    return lax.dot_general(a, b, (((1,), (1,)), ((), ())), preferred_element_type=F32)


def _dot_tn(a, b):
    return lax.dot_general(a, b, (((0,), (0,)), ((), ())), preferred_element_type=F32)


def _layer_norm(z, g, b):
    mu = jnp.mean(z, axis=-1, keepdims=True)
    zc = z - mu
    var = jnp.mean(zc * zc, axis=-1, keepdims=True)
    return zc * lax.rsqrt(var + LN_EPS) * g + b


def _gelu(x):
    return 0.5 * x * (1.0 + lax.erf(x * (2.0 ** -0.5)))


def _lower_bound(raw, layer):
    m = jnp.max(raw, axis=0, keepdims=True)
    e = jnp.exp(raw - m)
    p = e / jnp.sum(e, axis=0, keepdims=True)
    c = p[0:1]
    for i in range(1, layer + 1):
        c = c + p[i:i + 1]
    return c - p[0:1]


def _hgrn_gates(xb, win_ref, lb):
    d = D_MODEL
    q = jax.nn.silu(_dot(xb, win_ref[:, 0:d])) * (DK_A ** -0.5)
    f = lb + (1.0 - lb) * jax.nn.sigmoid(_dot(xb, win_ref[:, d:2 * d]))
    v = _dot(xb, win_ref[:, 2 * d:3 * d])
    gate = jax.nn.silu(_dot(xb, win_ref[:, 3 * d:4 * d]))
    return q, f, v, gate


def _split2(a):
    hi = a.astype(BF16)
    lo = (a - hi.astype(F32)).astype(BF16)
    return jnp.concatenate([hi, lo], axis=1)


def _level_ref(g, half):
    n, w = g.shape
    if half >= SUBLANES // 2:
        blk = 2 * half
        parts = [jnp.broadcast_to(g[b * blk + half - 1:b * blk + half, :], (blk, w))
                 for b in range(n // blk)]
        return parts[0] if len(parts) == 1 else jnp.concatenate(parts, axis=0)
    row = lax.broadcasted_iota(jnp.int32, g.shape, 0)
    if half == 2:
        m = row % 4
        return jnp.where(m == 0, pltpu.roll(g, n - 1, axis=0),
                         jnp.where(m == 1, g,
                                   jnp.where(m == 2, pltpu.roll(g, 1, axis=0),
                                             pltpu.roll(g, 2, axis=0))))
    assert half == 1
    return jnp.where(row % 2 == 0, g, pltpu.roll(g, 1, axis=0))


def _hgrn_group_chunk(heads, c, tri, q_s, k_s, v_s, lf_s, oh_s, st_ref, mask_ref, sign_ref,
                      norm_g):
    rows = pl.ds(pl.multiple_of(c * CHUNK_A, CHUNK_A), CHUNK_A)
    n = range(len(heads))
    q = [q_s[h, rows, :] for h in heads]
    k = [k_s[h, rows, :] for h in heads]
    vb = [v_s[h, rows, :].astype(BF16) for h in heads]
    st = [st_ref[h] for h in heads]
    g2 = [_dot(tri, lf_s[h, rows, :]) for h in heads]
    g = [a[:, :DK_A] + a[:, DK_A:] for a in g2]
    g_last = [a[CHUNK_A - 1:CHUNK_A, :] for a in g]
    o = [_dot_nt((q[i] * jnp.exp2(g[i])).astype(BF16), st[i].astype(BF16)) for i in n]
    ks = [(k[i] * jnp.exp2(g_last[i] - g[i])).astype(BF16) for i in n]
    st_new = [st[i] * jnp.exp2(g_last[i]) + _dot_tn(vb[i], ks[i]) for i in n]
    diag = mask_ref[len(LEVEL_HALVES)]
    a = [diag * _dot_nt(q[i].astype(BF16), k[i].astype(BF16)) for i in n]
    for lvl, half in enumerate(LEVEL_HALVES):
        e = [jnp.exp2((g[i] - _level_ref(g[i], half)) * sign_ref[lvl]) for i in n]
        p = [_dot_nt((q[i] * e[i]).astype(BF16), (k[i] * e[i]).astype(BF16)) for i in n]
        a = [a[i] + mask_ref[lvl] * p[i] for i in n]
    o = [o[i] + _dot(a[i].astype(BF16), vb[i]) for i in n]
    ms = [jnp.mean(o[i] * o[i], axis=-1, keepdims=True) for i in n]
    for i, h in enumerate(heads):
        st_ref[h] = st_new[i]
        oh_s[h, rows, :] = o[i] * lax.rsqrt(ms[i] + RMS_EPS) * norm_g


def _hgrn_prompt_kernel(layer, x_ref, lbraw_ref, win_ref, ng_ref, wout_ref, lng_ref, lnb_ref,
                        mask_ref, sign_ref, o_ref, sfin_ref, st_ref, q_s, k_s, v_s, lf_s, oh_s,
                        gate_s):
    t = pl.program_id(1)

    @pl.when(t == 0)
    def _():
        st_ref[...] = jnp.zeros_like(st_ref)

    x = x_ref[0]
    q, f, v, gate = _hgrn_gates(x.astype(BF16), win_ref, _lower_bound(lbraw_ref[...], layer))
    k = 1.0 - f
    lf = jnp.log2(f)
    gate_s[...] = gate
    for h in range(H_A):
        ls = slice(h * DK_A, (h + 1) * DK_A)
        q_s[h] = q[:, ls]
        k_s[h] = k[:, ls]
        v_s[h] = v[:, ls]
        lf_s[h] = _split2(lf[:, ls])

    norm_g = ng_ref[...]
    row = lax.broadcasted_iota(jnp.int32, (CHUNK_A, CHUNK_A), 0)
    col = lax.broadcasted_iota(jnp.int32, (CHUNK_A, CHUNK_A), 1)
    tri = (row >= col).astype(BF16)

    def group_body(hg, carry):
        heads = [hg * HEAD_GROUP + i for i in range(HEAD_GROUP)]

        def chunk_body(c, carry2):
            _hgrn_group_chunk(heads, c, tri, q_s, k_s, v_s, lf_s, oh_s, st_ref, mask_ref, sign_ref,
                              norm_g)
            return carry2
        return lax.fori_loop(0, ROW_TILE // CHUNK_A, chunk_body, carry)

    lax.fori_loop(0, H_A // HEAD_GROUP, group_body, 0)

    o = jnp.concatenate([oh_s[h] for h in range(H_A)], axis=1) * gate_s[...]
    y = _dot(o.astype(BF16), wout_ref[...])
    o_ref[0] = _layer_norm(ALPHA * x + y, lng_ref[...], lnb_ref[...])

    @pl.when(t == pl.num_programs(1) - 1)
    def _():
        for h in range(H_A):
            sfin_ref[0, h] = st_ref[h].T


def _const_spec(shape):
    nd = len(shape)
    return pl.BlockSpec(shape, lambda *_: (0,) * nd, pipeline_mode=pl.Buffered(1))


def _params(n_axes):
    return pltpu.CompilerParams(dimension_semantics=("arbitrary",) * n_axes,
                                vmem_limit_bytes=VMEM_LIMIT_BYTES)


def _hgrn_prompt(x, layer, lb_raw, w_in, norm_g, w_out, ln_g, ln_b, masks, signs):
    b, l, d = x.shape
    assert l % ROW_TILE == 0 and ROW_TILE % CHUNK_A == 0 and H_A % HEAD_GROUP == 0
    head_scratch = pltpu.VMEM((H_A, ROW_TILE, DK_A), F32)
    return pl.pallas_call(
        functools.partial(_hgrn_prompt_kernel, layer),
        grid=(b, l // ROW_TILE),
        in_specs=[
            pl.BlockSpec((1, ROW_TILE, d), lambda i, t: (i, t, 0)),
            _const_spec(lb_raw.shape),
            _const_spec(w_in.shape),
            _const_spec(norm_g.shape),
            _const_spec(w_out.shape),
            _const_spec(ln_g.shape),
            _const_spec(ln_b.shape),
            _const_spec(masks.shape),
            _const_spec(signs.shape),
        ],
        out_specs=[
            pl.BlockSpec((1, ROW_TILE, d), lambda i, t: (i, t, 0)),
            pl.BlockSpec((1, H_A, DK_A, DV_A), lambda i, t: (i, 0, 0, 0)),
        ],
        out_shape=[
            jax.ShapeDtypeStruct((b, l, d), F32),
            jax.ShapeDtypeStruct((b, H_A, DK_A, DV_A), F32),
        ],
        scratch_shapes=[
            pltpu.VMEM((H_A, DV_A, DK_A), F32),
            head_scratch, head_scratch, head_scratch,
            pltpu.VMEM((H_A, ROW_TILE, 2 * DK_A), BF16),
            head_scratch,
            pltpu.VMEM((ROW_TILE, d), F32),
        ],
        compiler_params=_params(2),
        name="hgrn_prompt",
    )(x, lb_raw, w_in, norm_g, w_out, ln_g, ln_b, masks, signs)


def _split3(f):
    hi = f.astype(BF16).astype(F32)
    r = f - hi
    mid = r.astype(BF16).astype(F32)
    lo = (r - mid).astype(BF16).astype(F32)
    return hi, mid, lo


def _hgrn_decode_kernel(layer, x_ref, lbraw_ref, win_ref, ng_ref, wout_ref, lng_ref, lnb_ref,
                        s_ref, o_ref, snew_ref, f3_s, k_s, q_s, v_s, gate_s, oacc_s):
    step = pl.program_id(0)
    d = D_MODEL
    nb = DEC_SEQS

    @pl.when(step == 0)
    def _():
        q, f, v, gate = _hgrn_gates(x_ref[...].astype(BF16), win_ref,
                                    _lower_bound(lbraw_ref[...], layer))
        hi, mid, lo = _split3(f)
        f3_s[0] = hi
        f3_s[1] = mid
        f3_s[2] = lo
        k_s[...] = 1.0 - f
        q_s[...] = q
        v_s[...] = v
        gate_s[...] = gate

    rows = pl.ds(pl.multiple_of(step * nb, nb), nb)
    fhi = f3_s[0, rows, :]
    fmid = f3_s[1, rows, :]
    flo = f3_s[2, rows, :]
    k8 = k_s[rows, :]
    q8 = q_s[rows, :]
    v8 = v_s[rows, :]
    ones = jnp.ones((nb, DV_A), F32)
    zeros = jnp.zeros((nb, DV_A), F32)
    rhs = []
    for h in range(H_A):
        ls = slice(h * DK_A, (h + 1) * DK_A)
        left = jnp.concatenate([ones, ones, ones, zeros], axis=0)
        right = jnp.concatenate([zeros, zeros, zeros, v8[:, ls]], axis=0)
        rhs.append(jnp.concatenate([left, right], axis=1).astype(BF16))
    seq = lax.broadcasted_iota(jnp.int32, (nb, DK_A), 0)

    def seq_body(i, o8):
        sel = seq == i
        outs = []
        for h in range(H_A):
            ls = slice(h * DK_A, (h + 1) * DK_A)
            lhs = jnp.concatenate([jnp.where(sel, a[:, ls], 0.0) for a in (fhi, fmid, flo, k8)],
                                  axis=0).astype(BF16)
            fb_kv = _dot_tn(lhs, rhs[h])
            s_new = fb_kv[:, :DV_A] * s_ref[i, h] + fb_kv[:, DV_A:]
            snew_ref[i, h] = s_new
            outs.append(_dot(jnp.where(sel, q8[:, ls], 0.0).astype(BF16), s_new.astype(BF16)))
        return o8 + jnp.concatenate(outs, axis=1)

    oacc_s[rows, :] = lax.fori_loop(0, nb, seq_body, jnp.zeros((nb, d), F32))

    @pl.when(step == pl.num_programs(0) - 1)
    def _():
        o = oacc_s[...]
        norm_g = ng_ref[...]
        heads = []
        for h in range(H_A):
            oh = o[:, h * DV_A:(h + 1) * DV_A]
            ms = jnp.mean(oh * oh, axis=-1, keepdims=True)
            heads.append(oh * lax.rsqrt(ms + RMS_EPS) * norm_g)
        on = jnp.concatenate(heads, axis=1) * gate_s[...]
        y = _dot(on.astype(BF16), wout_ref[...])
        o_ref[...] = _layer_norm(ALPHA * x_ref[...] + y, lng_ref[...], lnb_ref[...])


def _hgrn_decode(x, layer, lb_raw, w_in, norm_g, w_out, ln_g, ln_b, states, j):
    n, d = x.shape
    assert n % DEC_SEQS == 0
    row_scratch = pltpu.VMEM((n, d), F32)
    state_block = (None, DEC_SEQS, H_A, DK_A, DV_A)
    return pl.pallas_call(
        functools.partial(_hgrn_decode_kernel, layer),
        grid=(n // DEC_SEQS,),
        in_specs=[
            _const_spec(x.shape),
            _const_spec(lb_raw.shape),
            _const_spec(w_in.shape),
            _const_spec(norm_g.shape),
            _const_spec(w_out.shape),
            _const_spec(ln_g.shape),
            _const_spec(ln_b.shape),
            pl.BlockSpec(state_block, lambda s: (j, s, 0, 0, 0)),
        ],
        out_specs=[
            pl.BlockSpec((n, d), lambda s: (0, 0)),
            pl.BlockSpec(state_block[1:], lambda s: (s, 0, 0, 0)),
        ],
        out_shape=[
            jax.ShapeDtypeStruct((n, d), F32),
            jax.ShapeDtypeStruct(states.shape[1:], F32),
        ],
        scratch_shapes=[
            pltpu.VMEM((3, n, d), F32),
            row_scratch, row_scratch, row_scratch, row_scratch, row_scratch,
        ],
        compiler_params=_params(1),
        name="hgrn_decode",
    )(x, lb_raw, w_in, norm_g, w_out, ln_g, ln_b, states)


def _cmlp_kernel(single, x_ref, win_ref, vg_ref, vb_ref, ws_ref, bias_ref, wout_ref,
                 lng_ref, lnb_ref, o_ref, vrows_ref):
    x = x_ref[...]
    n = x.shape[0]
    xb = x.astype(BF16)
    di = D_INNER_B
    u = _gelu(_dot(xb, win_ref[:, 0:di]))
    v = _gelu(_dot(xb, win_ref[:, di:2 * di]))
    v = _layer_norm(v, vg_ref[...], vb_ref[...])
    if single:
        mixed = v * ws_ref[...] + bias_ref[...]
        vrows_ref[...] = v
    else:
        row = lax.broadcasted_iota(jnp.int32, (CHUNK_B, CHUNK_B), 0)
        col = lax.broadcasted_iota(jnp.int32, (CHUNK_B, CHUNK_B), 1)
        w = [jnp.where(row >= col, ws_ref[g], 0.0).astype(BF16) for g in range(G_B)]
        bias = bias_ref[...]
        vb16 = v.astype(BF16)
        chunks = []
        for c in range(n // CHUNK_B):
            rs = slice(c * CHUNK_B, (c + 1) * CHUNK_B)
            groups = [_dot(w[g], vb16[rs, g * DG_B:(g + 1) * DG_B]) for g in range(G_B)]
            chunks.append(jnp.concatenate(groups, axis=1) + bias)
        mixed = jnp.concatenate(chunks, axis=0)

        @pl.when(pl.program_id(1) == pl.num_programs(1) - 1)
        def _():
            vrows_ref[...] = v[n - CHUNK_B:, :]

    y = _dot((u * mixed).astype(BF16), wout_ref[...])
    o_ref[...] = _layer_norm(ALPHA * x + y, lng_ref[...], lnb_ref[...])


def _cmlp_prompt(x, w_in, v_g, v_b, w_s, bias_full, w_out, ln_g, ln_b):
    b, l, d = x.shape
    assert l % ROW_TILE == 0 and ROW_TILE % CHUNK_B == 0
    return pl.pallas_call(
        functools.partial(_cmlp_kernel, False),
        grid=(b, l // ROW_TILE),
        in_specs=[
            pl.BlockSpec((None, ROW_TILE, d), lambda i, t: (i, t, 0)),
            _const_spec(w_in.shape),
            _const_spec(v_g.shape),
            _const_spec(v_b.shape),
            _const_spec(w_s.shape),
            _const_spec(bias_full.shape),
            _const_spec(w_out.shape),
            _const_spec(ln_g.shape),
            _const_spec(ln_b.shape),
        ],
        out_specs=[
            pl.BlockSpec((None, ROW_TILE, d), lambda i, t: (i, t, 0)),
            pl.BlockSpec((None, CHUNK_B, D_INNER_B), lambda i, t: (i, 0, 0)),
        ],
        out_shape=[
            jax.ShapeDtypeStruct((b, l, d), F32),
            jax.ShapeDtypeStruct((b, CHUNK_B, D_INNER_B), F32),
        ],
        compiler_params=_params(2),
        name="cmlp_prompt",
    )(x, w_in, v_g, v_b, w_s, bias_full, w_out, ln_g, ln_b)


def _cmlp_decode(x, w_in, v_g, v_b, w_row, bias_row, w_out, ln_g, ln_b):
    n, d = x.shape
    operands = (x, w_in, v_g, v_b, w_row, bias_row, w_out, ln_g, ln_b)
    return pl.pallas_call(
        functools.partial(_cmlp_kernel, True),
        grid=(1,),
        in_specs=[_const_spec(a.shape) for a in operands],
        out_specs=[
            pl.BlockSpec((n, d), lambda s: (0, 0)),
            pl.BlockSpec((n, D_INNER_B), lambda s: (0, 0)),
        ],
        out_shape=[
            jax.ShapeDtypeStruct((n, d), F32),
            jax.ShapeDtypeStruct((n, D_INNER_B), F32),
        ],
        compiler_params=_params(1),
        name="cmlp_decode",
    )(*operands)


def _ffn_kernel(x_ref, win_ref, wout_ref, lng_ref, lnb_ref, o_ref):
    x = x_ref[...]
    xb = x.astype(BF16)
    acc = jnp.zeros(x.shape, F32)
    for j in range(D_FF // FF_TILE):
        cs = slice(j * FF_TILE, (j + 1) * FF_TILE)
        us = slice(D_FF + j * FF_TILE, D_FF + (j + 1) * FF_TILE)
        hcol = jax.nn.silu(_dot(xb, win_ref[:, cs])) * _dot(xb, win_ref[:, us])
        acc = acc + _dot(hcol.astype(BF16), wout_ref[cs, :])
    o_ref[...] = _layer_norm(ALPHA * x + acc, lng_ref[...], lnb_ref[...])


def _ffn(x, w_in, w_out, ln_g, ln_b, row_tile):
    n, d = x.shape
    assert n % row_tile == 0 and D_FF % FF_TILE == 0
    return pl.pallas_call(
        _ffn_kernel,
        grid=(n // row_tile,),
        in_specs=[
            pl.BlockSpec((row_tile, d), lambda i: (i, 0)),
            _const_spec(w_in.shape),
            _const_spec(w_out.shape),
            _const_spec(ln_g.shape),
            _const_spec(ln_b.shape),
        ],
        out_specs=pl.BlockSpec((row_tile, d), lambda i: (i, 0)),
        out_shape=jax.ShapeDtypeStruct((n, d), F32),
        compiler_params=_params(1),
        name="ffn",
    )(x, w_in, w_out, ln_g, ln_b)


def kernel(x_prompt, x_sample, state_hgrn, ln_mix_g, ln_mix_b, ln_ffn_g, ln_ffn_b, a_lb_raw, a_w_in, a_norm_g, a_w_out, b_w_in, b_ln_g, b_ln_b, b_w_s, b_bias_s, b_w_out, ffn_w_in, ffn_w_out):
    bsz, seq, d = x_prompt.shape
    n_dec = x_sample.shape[0]
    assert x_sample.shape[1] == 1 and seq % CHUNK_B == 0

    a_w_in_b = a_w_in.astype(BF16)
    a_w_out_b = a_w_out.astype(BF16)
    b_w_in_b = b_w_in.astype(BF16)
    b_w_out_b = b_w_out.astype(BF16)
    ffn_w_in_b = ffn_w_in.astype(BF16)
    ffn_w_out_b = ffn_w_out.astype(BF16)
    masks = jnp.asarray(_level_masks(CHUNK_A))
    signs = jnp.asarray(_level_signs(CHUNK_A, DK_A))
    lb_raw = a_lb_raw.astype(F32)
    bias_full = jnp.repeat(jnp.swapaxes(b_bias_s, 1, 2), DG_B, axis=2)
    w_row = jnp.repeat(b_w_s[:, :, 0, 0], DG_B, axis=1)[:, None, :]

    def row(p, i):
        return p[i][None, :]

    xp = x_prompt
    xs = x_sample.reshape(n_dec, d)
    st_p, st_s, v_p, v_s = [], [], [], []
    for layer in range(DEPTH):
        j = layer // N_MIXERS
        mg, mb = row(ln_mix_g, layer), row(ln_mix_b, layer)
        if layer % N_MIXERS == 0:
            ng = a_norm_g[j][None, :]
            xp, s = _hgrn_prompt(xp, layer, lb_raw, a_w_in_b[j], ng, a_w_out_b[j], mg, mb, masks,
                                 signs)
            st_p.append(s)
            xs, s = _hgrn_decode(xs, layer, lb_raw, a_w_in_b[j], ng, a_w_out_b[j], mg, mb,
                                 state_hgrn, j)
            st_s.append(s)
        else:
            vg, vb = row(b_ln_g, j), row(b_ln_b, j)
            xp, v = _cmlp_prompt(xp, b_w_in_b[j], vg, vb, b_w_s[j], bias_full[j], b_w_out_b[j],
                                 mg, mb)
            v_p.append(v)
            xs, v = _cmlp_decode(xs, b_w_in_b[j], vg, vb, w_row[j], bias_full[j][0:1],
                                 b_w_out_b[j], mg, mb)
            v_s.append(v[:, None, :])
        fg, fb = row(ln_ffn_g, layer), row(ln_ffn_b, layer)
        xp = _ffn(xp.reshape(bsz * seq, d), ffn_w_in_b[layer], ffn_w_out_b[layer], fg, fb,
                  ROW_TILE).reshape(bsz, seq, d)
        xs = _ffn(xs, ffn_w_in_b[layer], ffn_w_out_b[layer], fg, fb, n_dec)

    return (xp, xs.reshape(n_dec, 1, d), jnp.stack(st_p), jnp.stack(st_s),
            jnp.stack(v_p), jnp.stack(v_s))
```

```python
import functools

import numpy as np
import jax
import jax.numpy as jnp
from jax import lax
from jax.experimental import pallas as pl
from jax.experimental.pallas import tpu as pltpu

D_MODEL = 1024
DEPTH = 4
N_MIXERS = 2
DK_A = 128
H_A = D_MODEL // DK_A
DV_A = D_MODEL // H_A
D_INNER_B = D_MODEL
CHUNK_B = 128
G_B = 8
DG_B = D_INNER_B // G_B
D_FF = -(-(8 * D_MODEL) // (3 * 256)) * 256
ALPHA = (2 * DEPTH) ** 0.25
LN_EPS = 1e-5
RMS_EPS = 1e-6

LANES = 128
SUBLANES = 8
MXU_DIM = 256
VMEM_LIMIT_BYTES = 56 * 2**20

ROW_TILE = 512
FFN_ROW_TILE = 1024
CHUNK_A = 128
HEAD_GROUP = 8
FF_TILE = MXU_DIM
DEC_SEQS = 8
DEC_SEQ_GROUP = 2

BF16 = jnp.bfloat16
F32 = jnp.float32


def _level_halves(chunk):
    out, h = [], chunk // 2
    while h >= 1:
        out.append(h)
        h //= 2
    return tuple(out)


LEVEL_HALVES = _level_halves(CHUNK_A)


def _level_masks(chunk):
    t = np.arange(chunk)[:, None]
    s = np.arange(chunk)[None, :]
    masks = [((t // (2 * h) == s // (2 * h)) & (t % (2 * h) >= h) & (s % (2 * h) < h))
             for h in _level_halves(chunk)]
    masks.append(t == s)
    return np.stack(masks).astype(np.float32)


def _level_signs(chunk, width):
    t = np.arange(chunk)[:, None]
    signs = [np.where(t % (2 * h) >= h, 1.0, -1.0) for h in _level_halves(chunk)]
    return np.broadcast_to(np.stack(signs), (len(signs), chunk, width)).astype(np.float32)


def _dot(a, b):
    return jnp.dot(a, b, preferred_element_type=F32)


def _dot_nt(a, b):
    return lax.dot_general(a, b, (((1,), (1,)), ((), ())), preferred_element_type=F32)


def _dot_tn(a, b):
    return lax.dot_general(a, b, (((0,), (0,)), ((), ())), preferred_element_type=F32)


def _layer_norm(z, g, b):
    mu = jnp.mean(z, axis=-1, keepdims=True)
    zc = z - mu
    var = jnp.mean(zc * zc, axis=-1, keepdims=True)
    return zc * lax.rsqrt(var + LN_EPS) * g + b


def _gelu(x):
    return 0.5 * x * (1.0 + lax.erf(x * (2.0 ** -0.5)))


def _lower_bound(raw, layer):
    m = jnp.max(raw, axis=0, keepdims=True)
    e = jnp.exp(raw - m)
    p = e / jnp.sum(e, axis=0, keepdims=True)
    c = p[0:1]
    for i in range(1, layer + 1):
        c = c + p[i:i + 1]
    return c - p[0:1]


def _hgrn_gates(xb, win_ref, lb):
    d = D_MODEL
    q = jax.nn.silu(_dot(xb, win_ref[:, 0:d])) * (DK_A ** -0.5)
    f = lb + (1.0 - lb) * jax.nn.sigmoid(_dot(xb, win_ref[:, d:2 * d]))
    v = _dot(xb, win_ref[:, 2 * d:3 * d])
    gate = jax.nn.silu(_dot(xb, win_ref[:, 3 * d:4 * d]))
    return q, f, v, gate


def _split2(a):
    hi = a.astype(BF16)
    lo = (a - hi.astype(F32)).astype(BF16)
    return jnp.concatenate([hi, lo], axis=1)


def _level_ref(g, half):
    n, w = g.shape
    if half >= SUBLANES // 2:
        blk = 2 * half
        parts = [jnp.broadcast_to(g[b * blk + half - 1:b * blk + half, :], (blk, w))
                 for b in range(n // blk)]
        return parts[0] if len(parts) == 1 else jnp.concatenate(parts, axis=0)
    row = lax.broadcasted_iota(jnp.int32, g.shape, 0)
    if half == 2:
        m = row % 4
        return jnp.where(m == 0, pltpu.roll(g, n - 1, axis=0),
                         jnp.where(m == 1, g,
                                   jnp.where(m == 2, pltpu.roll(g, 1, axis=0),
                                             pltpu.roll(g, 2, axis=0))))
    assert half == 1
    return jnp.where(row % 2 == 0, g, pltpu.roll(g, 1, axis=0))


def _hgrn_group_chunk(heads, c, tri, q_s, k_s, v_s, lf_s, oh_s, st_ref, mask_ref, sign_ref,
                      norm_g):
    rows = pl.ds(pl.multiple_of(c * CHUNK_A, CHUNK_A), CHUNK_A)
    n = range(len(heads))
    q = [q_s[h, rows, :] for h in heads]
    k = [k_s[h, rows, :] for h in heads]
    vb = [v_s[h, rows, :].astype(BF16) for h in heads]
    st = [st_ref[h] for h in heads]
    g2 = [_dot(tri, lf_s[h, rows, :]) for h in heads]
    g = [a[:, :DK_A] + a[:, DK_A:] for a in g2]
    g_last = [a[CHUNK_A - 1:CHUNK_A, :] for a in g]
    o = [_dot_nt((q[i] * jnp.exp2(g[i])).astype(BF16), st[i].astype(BF16)) for i in n]
    ks = [(k[i] * jnp.exp2(g_last[i] - g[i])).astype(BF16) for i in n]
    st_new = [st[i] * jnp.exp2(g_last[i]) + _dot_tn(vb[i], ks[i]) for i in n]
    diag = mask_ref[len(LEVEL_HALVES)]
    a = [diag * _dot_nt(q[i].astype(BF16), k[i].astype(BF16)) for i in n]
    for lvl, half in enumerate(LEVEL_HALVES):
        e = [jnp.exp2((g[i] - _level_ref(g[i], half)) * sign_ref[lvl]) for i in n]
        p = [_dot_nt((q[i] * e[i]).astype(BF16), (k[i] * e[i]).astype(BF16)) for i in n]
        a = [a[i] + mask_ref[lvl] * p[i] for i in n]
    o = [o[i] + _dot(a[i].astype(BF16), vb[i]) for i in n]
    ms = [jnp.mean(o[i] * o[i], axis=-1, keepdims=True) for i in n]
    for i, h in enumerate(heads):
        st_ref[h] = st_new[i]
        oh_s[h, rows, :] = o[i] * lax.rsqrt(ms[i] + RMS_EPS) * norm_g


def _hgrn_prompt_kernel(layer, x_ref, lbraw_ref, win_ref, ng_ref, wout_ref, lng_ref, lnb_ref,
                        mask_ref, sign_ref, o_ref, sfin_ref, st_ref, q_s, k_s, v_s, lf_s, oh_s,
                        gate_s):
    t = pl.program_id(1)

    @pl.when(t == 0)
    def _():
        st_ref[...] = jnp.zeros_like(st_ref)

    x = x_ref[0]
    q, f, v, gate = _hgrn_gates(x.astype(BF16), win_ref, _lower_bound(lbraw_ref[...], layer))
    k = 1.0 - f
    lf = jnp.log2(f)
    gate_s[...] = gate
    for h in range(H_A):
        ls = slice(h * DK_A, (h + 1) * DK_A)
        q_s[h] = q[:, ls]
        k_s[h] = k[:, ls]
        v_s[h] = v[:, ls]
        lf_s[h] = _split2(lf[:, ls])

    norm_g = ng_ref[...]
    row = lax.broadcasted_iota(jnp.int32, (CHUNK_A, CHUNK_A), 0)
    col = lax.broadcasted_iota(jnp.int32, (CHUNK_A, CHUNK_A), 1)
    tri = (row >= col).astype(BF16)

    def group_body(hg, carry):
        heads = [hg * HEAD_GROUP + i for i in range(HEAD_GROUP)]

        def chunk_body(c, carry2):
            _hgrn_group_chunk(heads, c, tri, q_s, k_s, v_s, lf_s, oh_s, st_ref, mask_ref, sign_ref,
                              norm_g)
            return carry2
        return lax.fori_loop(0, ROW_TILE // CHUNK_A, chunk_body, carry)

    lax.fori_loop(0, H_A // HEAD_GROUP, group_body, 0)

    o = jnp.concatenate([oh_s[h] for h in range(H_A)], axis=1) * gate_s[...]
    y = _dot(o.astype(BF16), wout_ref[...])
    o_ref[0] = _layer_norm(ALPHA * x + y, lng_ref[...], lnb_ref[...])

    @pl.when(t == pl.num_programs(1) - 1)
    def _():
        for h in range(H_A):
            sfin_ref[0, h] = st_ref[h].T


def _const_spec(shape):
    nd = len(shape)
    return pl.BlockSpec(shape, lambda *_: (0,) * nd, pipeline_mode=pl.Buffered(1))


def _layer_spec(shape, j):
    nd = len(shape)
    return pl.BlockSpec((None,) + tuple(shape[1:]), lambda *_: (j,) + (0,) * (nd - 1),
                        pipeline_mode=pl.Buffered(1))


def _params(n_axes):
    return pltpu.CompilerParams(dimension_semantics=("arbitrary",) * n_axes,
                                vmem_limit_bytes=VMEM_LIMIT_BYTES)


def _hgrn_prompt(x, layer, lb_raw, w_in, norm_g, w_out, ln_g, ln_b, masks, signs):
    b, l, d = x.shape
    j = layer // N_MIXERS
    assert l % ROW_TILE == 0 and ROW_TILE % CHUNK_A == 0 and H_A % HEAD_GROUP == 0
    head_scratch = pltpu.VMEM((H_A, ROW_TILE, DK_A), F32)
    return pl.pallas_call(
        functools.partial(_hgrn_prompt_kernel, layer),
        grid=(b, l // ROW_TILE),
        in_specs=[
            pl.BlockSpec((1, ROW_TILE, d), lambda i, t: (i, t, 0)),
            _const_spec(lb_raw.shape),
            _layer_spec(w_in.shape, j),
            _layer_spec(norm_g.shape, j),
            _layer_spec(w_out.shape, j),
            _layer_spec(ln_g.shape, layer),
            _layer_spec(ln_b.shape, layer),
            _const_spec(masks.shape),
            _const_spec(signs.shape),
        ],
        out_specs=[
            pl.BlockSpec((1, ROW_TILE, d), lambda i, t: (i, t, 0)),
            pl.BlockSpec((1, H_A, DK_A, DV_A), lambda i, t: (i, 0, 0, 0)),
        ],
        out_shape=[
            jax.ShapeDtypeStruct((b, l, d), F32),
            jax.ShapeDtypeStruct((b, H_A, DK_A, DV_A), F32),
        ],
        scratch_shapes=[
            pltpu.VMEM((H_A, DV_A, DK_A), F32),
            head_scratch, head_scratch, head_scratch,
            pltpu.VMEM((H_A, ROW_TILE, 2 * DK_A), BF16),
            head_scratch,
            pltpu.VMEM((ROW_TILE, d), F32),
        ],
        compiler_params=_params(2),
        name="hgrn_prompt",
    )(x, lb_raw, w_in, norm_g, w_out, ln_g, ln_b, masks, signs)


def _split3(f):
    hi = f.astype(BF16).astype(F32)
    r = f - hi
    mid = r.astype(BF16).astype(F32)
    lo = (r - mid).astype(BF16).astype(F32)
    return hi, mid, lo


def _hgrn_decode_kernel(layer, chained, x_ref, lbraw_ref, win_ref, ng_ref, wout_ref, lng_ref,
                        lnb_ref, s_ref, *rest):
    o_ref, snew_ref, f3_s, k_s, q_s, v_s, gate_s, oacc_s = rest[1:] if chained else rest

    @pl.when(pl.program_id(0) == 0)
    def _():
        _hgrn_decode_step(layer, x_ref, lbraw_ref, win_ref, ng_ref, wout_ref, lng_ref, lnb_ref,
                          s_ref, o_ref, snew_ref, f3_s, k_s, q_s, v_s, gate_s, oacc_s)

    @pl.when(pl.program_id(0) > 0)
    def _():
        snew_ref[...] = jnp.zeros_like(snew_ref)


def _hgrn_decode_step(layer, x_ref, lbraw_ref, win_ref, ng_ref, wout_ref, lng_ref, lnb_ref,
                      s_ref, o_ref, snew_ref, f3_s, k_s, q_s, v_s, gate_s, oacc_s):
    step = pl.program_id(1)
    d = D_MODEL
    nb = DEC_SEQS

    @pl.when(step == 0)
    def _():
        q, f, v, gate = _hgrn_gates(x_ref[...].astype(BF16), win_ref,
                                    _lower_bound(lbraw_ref[...], layer))
        hi, mid, lo = _split3(f)
        f3_s[0] = hi
        f3_s[1] = mid
        f3_s[2] = lo
        k_s[...] = 1.0 - f
        q_s[...] = q
        v_s[...] = v
        gate_s[...] = gate

    rows = pl.ds(pl.multiple_of(step * nb, nb), nb)
    fhi = f3_s[0, rows, :]
    fmid = f3_s[1, rows, :]
    flo = f3_s[2, rows, :]
    k8 = k_s[rows, :]
    q8 = q_s[rows, :]
    v8 = v_s[rows, :]
    ones = jnp.ones((nb, DV_A), F32)
    zeros = jnp.zeros((nb, DV_A), F32)
    rhs = []
    for h in range(H_A):
        ls = slice(h * DK_A, (h + 1) * DK_A)
        left = jnp.concatenate([ones, ones, ones, zeros], axis=0)
        right = jnp.concatenate([zeros, zeros, zeros, v8[:, ls]], axis=0)
        rhs.append(jnp.concatenate([left, right], axis=1).astype(BF16))
    seq = lax.broadcasted_iota(jnp.int32, (nb, DK_A), 0)
    head_lanes = [slice(h * DK_A, (h + 1) * DK_A) for h in range(H_A)]

    def group_body(p, o8):
        seqs = [p * DEC_SEQ_GROUP + i for i in range(DEC_SEQ_GROUP)]
        sels = [seq == i for i in seqs]
        chains = [(si, h) for si in range(DEC_SEQ_GROUP) for h in range(H_A)]
        s_old = [s_ref[seqs[si], h] for si, h in chains]
        lhs = [jnp.concatenate([jnp.where(sels[si], a[:, head_lanes[h]], 0.0)
                                for a in (fhi, fmid, flo, k8)], axis=0).astype(BF16)
               for si, h in chains]
        fb_kv = [_dot_tn(lhs[c], rhs[h]) for c, (si, h) in enumerate(chains)]
        s_new = [fb_kv[c][:, :DV_A] * s_old[c] + fb_kv[c][:, DV_A:] for c in range(len(chains))]
        qsel = [jnp.where(sels[si], q8[:, head_lanes[h]], 0.0).astype(BF16) for si, h in chains]
        outs = [_dot(qsel[c], s_new[c].astype(BF16)) for c in range(len(chains))]
        for c, (si, h) in enumerate(chains):
            snew_ref[seqs[si], h] = s_new[c]
        for si in range(DEC_SEQ_GROUP):
            o8 = o8 + jnp.concatenate(outs[si * H_A:(si + 1) * H_A], axis=1)
        return o8

    oacc_s[rows, :] = lax.fori_loop(0, nb // DEC_SEQ_GROUP, group_body, jnp.zeros((nb, d), F32))

    @pl.when(step == pl.num_programs(1) - 1)
    def _():
        o = oacc_s[...]
        norm_g = ng_ref[...]
        heads = []
        for h in range(H_A):
            oh = o[:, h * DV_A:(h + 1) * DV_A]
            ms = jnp.mean(oh * oh, axis=-1, keepdims=True)
            heads.append(oh * lax.rsqrt(ms + RMS_EPS) * norm_g)
        on = jnp.concatenate(heads, axis=1) * gate_s[...]
        y = _dot(on.astype(BF16), wout_ref[...])
        o_ref[...] = _layer_norm(ALPHA * x_ref[...] + y, lng_ref[...], lnb_ref[...])


def _hgrn_decode(x, layer, lb_raw, w_in, norm_g, w_out, ln_g, ln_b, states, new_states):
    n, d = x.shape
    j = layer // N_MIXERS
    assert n % DEC_SEQS == 0 and DEC_SEQS % DEC_SEQ_GROUP == 0
    chained = new_states is not None
    steps = n // DEC_SEQS
    passes = 1 if chained else states.shape[0] - j
    row_scratch = pltpu.VMEM((n, d), F32)
    state_block = (None, DEC_SEQS, H_A, DK_A, DV_A)
    state_in = pl.BlockSpec(state_block,
                            lambda g, s: (j, jnp.where(g == 0, s, steps - 1), 0, 0, 0))
    state_out = pl.BlockSpec(state_block, lambda g, s: (j + g, s, 0, 0, 0))
    operands = [x, lb_raw, w_in, norm_g, w_out, ln_g, ln_b, states]
    in_specs = [
        _const_spec(x.shape),
        _const_spec(lb_raw.shape),
        _layer_spec(w_in.shape, j),
        _layer_spec(norm_g.shape, j),
        _layer_spec(w_out.shape, j),
        _layer_spec(ln_g.shape, layer),
        _layer_spec(ln_b.shape, layer),
        state_in,
    ]
    if chained:
        operands.append(new_states)
        in_specs.append(pl.BlockSpec(memory_space=pl.ANY))
    return pl.pallas_call(
        functools.partial(_hgrn_decode_kernel, layer, chained),
        grid=(passes, steps),
        in_specs=in_specs,
        out_specs=[pl.BlockSpec((n, d), lambda g, s: (0, 0)), state_out],
        out_shape=[
            jax.ShapeDtypeStruct((n, d), F32),
            jax.ShapeDtypeStruct(states.shape, F32),
        ],
        scratch_shapes=[
            pltpu.VMEM((3, n, d), F32),
            row_scratch, row_scratch, row_scratch, row_scratch, row_scratch,
        ],
        input_output_aliases={len(operands) - 1: 1} if chained else {},
        compiler_params=_params(2),
        name="hgrn_decode",
    )(*operands)


def _cmlp_gate_values(x, win_ref, vg_ref, vb_ref):
    xb = x.astype(BF16)
    di = D_INNER_B
    u = _gelu(_dot(xb, win_ref[:, 0:di]))
    v = _gelu(_dot(xb, win_ref[:, di:2 * di]))
    return u, _layer_norm(v, vg_ref[...], vb_ref[...])


def _cmlp_finish(x, u, mixed, wout_ref, lng_ref, lnb_ref):
    y = _dot((u * mixed).astype(BF16), wout_ref[...])
    return _layer_norm(ALPHA * x + y, lng_ref[...], lnb_ref[...])


def _cmlp_kernel(x_ref, xdec_ref, win_ref, vg_ref, vb_ref, ws_ref, bias_ref, wout_ref,
                 lng_ref, lnb_ref, o_ref, vrows_ref, odec_ref, vdec_ref):
    x = x_ref[...]
    n = x.shape[0]
    u, v = _cmlp_gate_values(x, win_ref, vg_ref, vb_ref)
    row = lax.broadcasted_iota(jnp.int32, (CHUNK_B, CHUNK_B), 0)
    col = lax.broadcasted_iota(jnp.int32, (CHUNK_B, CHUNK_B), 1)
    w = [jnp.where(row >= col, ws_ref[g], 0.0).astype(BF16) for g in range(G_B)]
    bias = bias_ref[...]
    vb16 = v.astype(BF16)
    chunks = []
    for c in range(n // CHUNK_B):
        rs = slice(c * CHUNK_B, (c + 1) * CHUNK_B)
        groups = [_dot(w[g], vb16[rs, g * DG_B:(g + 1) * DG_B]) for g in range(G_B)]
        chunks.append(jnp.concatenate(groups, axis=1) + bias)
    o_ref[...] = _cmlp_finish(x, u, jnp.concatenate(chunks, axis=0), wout_ref, lng_ref, lnb_ref)

    @pl.when(pl.program_id(1) == pl.num_programs(1) - 1)
    def _():
        vrows_ref[...] = v[n - CHUNK_B:, :]

    @pl.when((pl.program_id(0) == 0) & (pl.program_id(1) == 0))
    def _():
        xd = xdec_ref[...]
        ud, vd = _cmlp_gate_values(xd, win_ref, vg_ref, vb_ref)
        w00 = jnp.concatenate([jnp.broadcast_to(ws_ref[g, 0:1, 0:1], (1, DG_B))
                               for g in range(G_B)], axis=1)
        vdec_ref[...] = vd
        odec_ref[...] = _cmlp_finish(xd, ud, vd * w00 + bias[0:1, :], wout_ref, lng_ref, lnb_ref)


def _cmlp(x, x_dec, j, layer, w_in, v_g, v_b, w_s, bias_full, w_out, ln_g, ln_b):
    b, l, d = x.shape
    assert l % ROW_TILE == 0 and ROW_TILE % CHUNK_B == 0
    dec_out = pl.BlockSpec(x_dec.shape, lambda i, t: (0, 0))
    return pl.pallas_call(
        _cmlp_kernel,
        grid=(b, l // ROW_TILE),
        in_specs=[
            pl.BlockSpec((None, ROW_TILE, d), lambda i, t: (i, t, 0)),
            _const_spec(x_dec.shape),
            _layer_spec(w_in.shape, j),
            _layer_spec(v_g.shape, j),
            _layer_spec(v_b.shape, j),
            _layer_spec(w_s.shape, j),
            _layer_spec(bias_full.shape, j),
            _layer_spec(w_out.shape, j),
            _layer_spec(ln_g.shape, layer),
            _layer_spec(ln_b.shape, layer),
        ],
        out_specs=[
            pl.BlockSpec((None, ROW_TILE, d), lambda i, t: (i, t, 0)),
            pl.BlockSpec((None, CHUNK_B, D_INNER_B), lambda i, t: (i, 0, 0)),
            dec_out,
            dec_out,
        ],
        out_shape=[
            jax.ShapeDtypeStruct((b, l, d), F32),
            jax.ShapeDtypeStruct((b, CHUNK_B, D_INNER_B), F32),
            jax.ShapeDtypeStruct(x_dec.shape, F32),
            jax.ShapeDtypeStruct(x_dec.shape, F32),
        ],
        compiler_params=_params(2),
        name="cmlp",
    )(x, x_dec, w_in, v_g, v_b, w_s, bias_full, w_out, ln_g, ln_b)


def _ffn_rows(x, win_ref, wout_ref, ln_g, ln_b):
    xb = x.astype(BF16)
    acc = jnp.zeros(x.shape, F32)
    for j in range(D_FF // FF_TILE):
        cs = slice(j * FF_TILE, (j + 1) * FF_TILE)
        us = slice(D_FF + j * FF_TILE, D_FF + (j + 1) * FF_TILE)
        hcol = jax.nn.silu(_dot(xb, win_ref[:, cs])) * _dot(xb, win_ref[:, us])
        acc = acc + _dot(hcol.astype(BF16), wout_ref[cs, :])
    return _layer_norm(ALPHA * x + acc, ln_g, ln_b)


def _ffn_kernel(x_ref, xdec_ref, win_ref, wout_ref, lng_ref, lnb_ref, o_ref, odec_ref):
    ln_g, ln_b = lng_ref[...], lnb_ref[...]
    o_ref[...] = _ffn_rows(x_ref[...], win_ref, wout_ref, ln_g, ln_b)

    @pl.when(pl.program_id(0) == 0)
    def _():
        odec_ref[...] = _ffn_rows(xdec_ref[...], win_ref, wout_ref, ln_g, ln_b)


def _ffn(x, x_dec, layer, w_in, w_out, ln_g, ln_b):
    n, d = x.shape
    assert n % FFN_ROW_TILE == 0 and D_FF % FF_TILE == 0
    return pl.pallas_call(
        _ffn_kernel,
        grid=(n // FFN_ROW_TILE,),
        in_specs=[
            pl.BlockSpec((FFN_ROW_TILE, d), lambda i: (i, 0)),
            _const_spec(x_dec.shape),
            _layer_spec(w_in.shape, layer),
            _layer_spec(w_out.shape, layer),
            _layer_spec(ln_g.shape, layer),
            _layer_spec(ln_b.shape, layer),
        ],
        out_specs=[
            pl.BlockSpec((FFN_ROW_TILE, d), lambda i: (i, 0)),
            pl.BlockSpec(x_dec.shape, lambda i: (0, 0)),
        ],
        out_shape=[
            jax.ShapeDtypeStruct((n, d), F32),
            jax.ShapeDtypeStruct(x_dec.shape, F32),
        ],
        compiler_params=_params(1),
        name="ffn",
    )(x, x_dec, w_in, w_out, ln_g, ln_b)


def kernel(x_prompt, x_sample, state_hgrn, ln_mix_g, ln_mix_b, ln_ffn_g, ln_ffn_b, a_lb_raw, a_w_in, a_norm_g, a_w_out, b_w_in, b_ln_g, b_ln_b, b_w_s, b_bias_s, b_w_out, ffn_w_in, ffn_w_out):
    bsz, seq, d = x_prompt.shape
    n_dec = x_sample.shape[0]
    assert x_sample.shape[1] == 1 and seq % CHUNK_B == 0

    a_w_in_b = a_w_in.astype(BF16)
    a_w_out_b = a_w_out.astype(BF16)
    b_w_in_b = b_w_in.astype(BF16)
    b_w_out_b = b_w_out.astype(BF16)
    ffn_w_in_b = ffn_w_in.astype(BF16)
    ffn_w_out_b = ffn_w_out.astype(BF16)
    masks = jnp.asarray(_level_masks(CHUNK_A))
    signs = jnp.asarray(_level_signs(CHUNK_A, DK_A))
    lb_raw = a_lb_raw.astype(F32)
    bias_full = jnp.repeat(jnp.swapaxes(b_bias_s, 1, 2), DG_B, axis=2)

    def rows(p):
        return p[:, None, :]

    mix_g, mix_b, ffn_g, ffn_b = rows(ln_mix_g), rows(ln_mix_b), rows(ln_ffn_g), rows(ln_ffn_b)
    norm_g, v_g, v_b = rows(a_norm_g), rows(b_ln_g), rows(b_ln_b)

    xp = x_prompt
    xs = x_sample.reshape(n_dec, d)
    st_p, v_p, v_s = [], [], []
    st_s = None
    for layer in range(DEPTH):
        j = layer // N_MIXERS
        if layer % N_MIXERS == 0:
            xp, s = _hgrn_prompt(xp, layer, lb_raw, a_w_in_b, norm_g, a_w_out_b, mix_g, mix_b,
                                 masks, signs)
            st_p.append(s)
            xs, st_s = _hgrn_decode(xs, layer, lb_raw, a_w_in_b, norm_g, a_w_out_b, mix_g, mix_b,
                                    state_hgrn, st_s)
        else:
            xp, v, xs, vd = _cmlp(xp, xs, j, layer, b_w_in_b, v_g, v_b, b_w_s, bias_full, b_w_out_b,
                                  mix_g, mix_b)
            v_p.append(v)
            v_s.append(vd[:, None, :])
        xp, xs = _ffn(xp.reshape(bsz * seq, d), xs, layer, ffn_w_in_b, ffn_w_out_b, ffn_g, ffn_b)
        xp = xp.reshape(bsz, seq, d)

    return (xp, xs.reshape(n_dec, 1, d), jnp.stack(st_p), st_s, jnp.stack(v_p), jnp.stack(v_s))
```

```python
import functools

import numpy as np
import jax
import jax.numpy as jnp
from jax import lax
from jax.experimental import pallas as pl
from jax.experimental.pallas import tpu as pltpu

D_MODEL = 1024
DEPTH = 4
N_MIXERS = 2
DK_A = 128
H_A = D_MODEL // DK_A
DV_A = D_MODEL // H_A
D_INNER_B = D_MODEL
CHUNK_B = 128
G_B = 8
DG_B = D_INNER_B // G_B
D_FF = -(-(8 * D_MODEL) // (3 * 256)) * 256
ALPHA = (2 * DEPTH) ** 0.25
LN_EPS = 1e-5
RMS_EPS = 1e-6

LANES = 128
SUBLANES = 8
MXU_DIM = 256
VMEM_LIMIT_BYTES = 56 * 2**20

ROW_TILE = 512
FFN_ROW_TILE = 1024
CHUNK_A = 128
HEAD_GROUP = 8
SUB_A = 16
DECAY_LIMIT_LOG2 = 64.0
FF_TILE = MXU_DIM
DEC_SEQS = 8
DEC_SEQ_GROUP = 2

BF16 = jnp.bfloat16
F32 = jnp.float32


def _level_halves(chunk):
    out, h = [], chunk // 2
    while h >= 1:
        out.append(h)
        h //= 2
    return tuple(out)


LEVEL_HALVES = _level_halves(CHUNK_A)


def _level_masks(chunk):
    t = np.arange(chunk)[:, None]
    s = np.arange(chunk)[None, :]
    masks = [((t // (2 * h) == s // (2 * h)) & (t % (2 * h) >= h) & (s % (2 * h) < h))
             for h in _level_halves(chunk)]
    masks.append(t == s)
    masks.append((t // SUB_A == s // SUB_A) & (t >= s))
    return np.stack(masks).astype(np.float32)


def _level_signs(chunk, width):
    t = np.arange(chunk)[:, None]
    signs = [np.where(t % (2 * h) >= h, 1.0, -1.0) for h in _level_halves(chunk)]
    return np.broadcast_to(np.stack(signs), (len(signs), chunk, width)).astype(np.float32)


def _dot(a, b):
    return jnp.dot(a, b, preferred_element_type=F32)


def _dot_nt(a, b):
    return lax.dot_general(a, b, (((1,), (1,)), ((), ())), preferred_element_type=F32)


def _dot_tn(a, b):
    return lax.dot_general(a, b, (((0,), (0,)), ((), ())), preferred_element_type=F32)


def _layer_norm(z, g, b):
    mu = jnp.mean(z, axis=-1, keepdims=True)
    zc = z - mu
    var = jnp.mean(zc * zc, axis=-1, keepdims=True)
    return zc * lax.rsqrt(var + LN_EPS) * g + b


def _gelu(x):
    return 0.5 * x * (1.0 + lax.erf(x * (2.0 ** -0.5)))


def _lower_bound(raw, layer):
    m = jnp.max(raw, axis=0, keepdims=True)
    e = jnp.exp(raw - m)
    p = e / jnp.sum(e, axis=0, keepdims=True)
    c = p[0:1]
    for i in range(1, layer + 1):
        c = c + p[i:i + 1]
    return c - p[0:1]


def _hgrn_gates(xb, win_ref, lb):
    d = D_MODEL
    q = jax.nn.silu(_dot(xb, win_ref[:, 0:d])) * (DK_A ** -0.5)
    f = lb + (1.0 - lb) * jax.nn.sigmoid(_dot(xb, win_ref[:, d:2 * d]))
    v = _dot(xb, win_ref[:, 2 * d:3 * d])
    gate = jax.nn.silu(_dot(xb, win_ref[:, 3 * d:4 * d]))
    return q, f, v, gate


def _split2(a):
    hi = a.astype(BF16)
    lo = (a - hi.astype(F32)).astype(BF16)
    return jnp.concatenate([hi, lo], axis=1)


def _level_ref(g, half):
    n, w = g.shape
    if half >= SUBLANES // 2:
        blk = 2 * half
        parts = [jnp.broadcast_to(g[b * blk + half - 1:b * blk + half, :], (blk, w))
                 for b in range(n // blk)]
        return parts[0] if len(parts) == 1 else jnp.concatenate(parts, axis=0)
    row = lax.broadcasted_iota(jnp.int32, g.shape, 0)
    if half == 2:
        m = row % 4
        return jnp.where(m == 0, pltpu.roll(g, n - 1, axis=0),
                         jnp.where(m == 1, g,
                                   jnp.where(m == 2, pltpu.roll(g, 1, axis=0),
                                             pltpu.roll(g, 2, axis=0))))
    assert half == 1
    return jnp.where(row % 2 == 0, g, pltpu.roll(g, 1, axis=0))


def _sub_block_ref(g):
    n, w = g.shape
    parts = [jnp.zeros((SUB_A, w), g.dtype)]
    parts += [jnp.broadcast_to(g[b * SUB_A - 1:b * SUB_A, :], (SUB_A, w))
              for b in range(1, n // SUB_A)]
    return jnp.concatenate(parts, axis=0)


def _hgrn_group_chunk(stable, heads, c, tri, q_s, k_s, v_s, lf_s, oh_s, st_ref, mask_ref, sign_ref,
                      norm_g):
    rows = pl.ds(pl.multiple_of(c * CHUNK_A, CHUNK_A), CHUNK_A)
    n = range(len(heads))
    q = [q_s[h, rows, :] for h in heads]
    k = [k_s[h, rows, :] for h in heads]
    vb = [v_s[h, rows, :].astype(BF16) for h in heads]
    st = [st_ref[h] for h in heads]
    g2 = [_dot(tri, lf_s[h, rows, :]) for h in heads]
    g = [a[:, :DK_A] + a[:, DK_A:] for a in g2]
    g_last = [a[CHUNK_A - 1:CHUNK_A, :] for a in g]
    o = [_dot_nt((q[i] * jnp.exp2(g[i])).astype(BF16), st[i].astype(BF16)) for i in n]
    ks = [(k[i] * jnp.exp2(g_last[i] - g[i])).astype(BF16) for i in n]
    st_new = [st[i] * jnp.exp2(g_last[i]) + _dot_tn(vb[i], ks[i]) for i in n]
    if stable:
        levels = [(lvl, half) for lvl, half in enumerate(LEVEL_HALVES) if 2 * half > SUB_A]
        near = mask_ref[len(LEVEL_HALVES) + 1]
        d = [g[i] - _sub_block_ref(g[i]) for i in n]
        a = [near * _dot_nt((q[i] * jnp.exp2(d[i])).astype(BF16),
                            (k[i] * jnp.exp2(-d[i])).astype(BF16)) for i in n]
    else:
        levels = list(enumerate(LEVEL_HALVES))
        diag = mask_ref[len(LEVEL_HALVES)]
        a = [diag * _dot_nt(q[i].astype(BF16), k[i].astype(BF16)) for i in n]
    for lvl, half in levels:
        e = [jnp.exp2((g[i] - _level_ref(g[i], half)) * sign_ref[lvl]) for i in n]
        p = [_dot_nt((q[i] * e[i]).astype(BF16), (k[i] * e[i]).astype(BF16)) for i in n]
        a = [a[i] + mask_ref[lvl] * p[i] for i in n]
    o = [o[i] + _dot(a[i].astype(BF16), vb[i]) for i in n]
    ms = [jnp.mean(o[i] * o[i], axis=-1, keepdims=True) for i in n]
    for i, h in enumerate(heads):
        st_ref[h] = st_new[i]
        oh_s[h, rows, :] = o[i] * lax.rsqrt(ms[i] + RMS_EPS) * norm_g


def _hgrn_prompt_kernel(layer, x_ref, lbraw_ref, win_ref, ng_ref, wout_ref, lng_ref, lnb_ref,
                        mask_ref, sign_ref, o_ref, sfin_ref, st_ref, q_s, k_s, v_s, lf_s, oh_s,
                        gate_s):
    t = pl.program_id(1)

    @pl.when(t == 0)
    def _():
        st_ref[...] = jnp.zeros_like(st_ref)

    x = x_ref[0]
    q, f, v, gate = _hgrn_gates(x.astype(BF16), win_ref, _lower_bound(lbraw_ref[...], layer))
    k = 1.0 - f
    lf = jnp.log2(f)
    gate_s[...] = gate
    for h in range(H_A):
        ls = slice(h * DK_A, (h + 1) * DK_A)
        q_s[h] = q[:, ls]
        k_s[h] = k[:, ls]
        v_s[h] = v[:, ls]
        lf_s[h] = _split2(lf[:, ls])

    norm_g = ng_ref[...]
    row = lax.broadcasted_iota(jnp.int32, (CHUNK_A, CHUNK_A), 0)
    col = lax.broadcasted_iota(jnp.int32, (CHUNK_A, CHUNK_A), 1)
    tri = (row >= col).astype(BF16)

    sub_decay = jnp.sum(lf.reshape(ROW_TILE // SUB_A, SUB_A, D_MODEL), axis=1)
    stable = jnp.min(sub_decay) >= -DECAY_LIMIT_LOG2

    def run_chunks(is_stable):
        def group_body(hg, carry):
            heads = [hg * HEAD_GROUP + i for i in range(HEAD_GROUP)]

            def chunk_body(c, carry2):
                _hgrn_group_chunk(is_stable, heads, c, tri, q_s, k_s, v_s, lf_s, oh_s, st_ref,
                                  mask_ref, sign_ref, norm_g)
                return carry2
            return lax.fori_loop(0, ROW_TILE // CHUNK_A, chunk_body, carry)

        lax.fori_loop(0, H_A // HEAD_GROUP, group_body, 0)

    lax.cond(stable, lambda: run_chunks(True), lambda: run_chunks(False))

    o = jnp.concatenate([oh_s[h] for h in range(H_A)], axis=1) * gate_s[...]
    y = _dot(o.astype(BF16), wout_ref[...])
    o_ref[0] = _layer_norm(ALPHA * x + y, lng_ref[...], lnb_ref[...])

    @pl.when(t == pl.num_programs(1) - 1)
    def _():
        for h in range(H_A):
            sfin_ref[0, h] = st_ref[h].T


def _const_spec(shape):
    nd = len(shape)
    return pl.BlockSpec(shape, lambda *_: (0,) * nd, pipeline_mode=pl.Buffered(1))


def _layer_spec(shape, j):
    nd = len(shape)
    return pl.BlockSpec((None,) + tuple(shape[1:]), lambda *_: (j,) + (0,) * (nd - 1),
                        pipeline_mode=pl.Buffered(1))


def _params(n_axes):
    return pltpu.CompilerParams(dimension_semantics=("arbitrary",) * n_axes,
                                vmem_limit_bytes=VMEM_LIMIT_BYTES)


def _hgrn_prompt(x, layer, lb_raw, w_in, norm_g, w_out, ln_g, ln_b, masks, signs):
    b, l, d = x.shape
    j = layer // N_MIXERS
    assert l % ROW_TILE == 0 and ROW_TILE % CHUNK_A == 0 and H_A % HEAD_GROUP == 0
    head_scratch = pltpu.VMEM((H_A, ROW_TILE, DK_A), F32)
    return pl.pallas_call(
        functools.partial(_hgrn_prompt_kernel, layer),
        grid=(b, l // ROW_TILE),
        in_specs=[
            pl.BlockSpec((1, ROW_TILE, d), lambda i, t: (i, t, 0)),
            _const_spec(lb_raw.shape),
            _layer_spec(w_in.shape, j),
            _layer_spec(norm_g.shape, j),
            _layer_spec(w_out.shape, j),
            _layer_spec(ln_g.shape, layer),
            _layer_spec(ln_b.shape, layer),
            _const_spec(masks.shape),
            _const_spec(signs.shape),
        ],
        out_specs=[
            pl.BlockSpec((1, ROW_TILE, d), lambda i, t: (i, t, 0)),
            pl.BlockSpec((1, H_A, DK_A, DV_A), lambda i, t: (i, 0, 0, 0)),
        ],
        out_shape=[
            jax.ShapeDtypeStruct((b, l, d), F32),
            jax.ShapeDtypeStruct((b, H_A, DK_A, DV_A), F32),
        ],
        scratch_shapes=[
            pltpu.VMEM((H_A, DV_A, DK_A), F32),
            head_scratch, head_scratch, head_scratch,
            pltpu.VMEM((H_A, ROW_TILE, 2 * DK_A), BF16),
            head_scratch,
            pltpu.VMEM((ROW_TILE, d), F32),
        ],
        compiler_params=_params(2),
        name="hgrn_prompt",
    )(x, lb_raw, w_in, norm_g, w_out, ln_g, ln_b, masks, signs)


def _split3(f):
    hi = f.astype(BF16).astype(F32)
    r = f - hi
    mid = r.astype(BF16).astype(F32)
    lo = (r - mid).astype(BF16).astype(F32)
    return hi, mid, lo


def _hgrn_decode_kernel(layer, chained, x_ref, lbraw_ref, win_ref, ng_ref, wout_ref, lng_ref,
                        lnb_ref, s_ref, *rest):
    o_ref, snew_ref, f3_s, k_s, q_s, v_s, gate_s, oacc_s = rest[1:] if chained else rest

    @pl.when(pl.program_id(0) == 0)
    def _():
        _hgrn_decode_step(layer, x_ref, lbraw_ref, win_ref, ng_ref, wout_ref, lng_ref, lnb_ref,
                          s_ref, o_ref, snew_ref, f3_s, k_s, q_s, v_s, gate_s, oacc_s)

    @pl.when(pl.program_id(0) > 0)
    def _():
        snew_ref[...] = jnp.zeros_like(snew_ref)


def _hgrn_decode_step(layer, x_ref, lbraw_ref, win_ref, ng_ref, wout_ref, lng_ref, lnb_ref,
                      s_ref, o_ref, snew_ref, f3_s, k_s, q_s, v_s, gate_s, oacc_s):
    step = pl.program_id(1)
    d = D_MODEL
    nb = DEC_SEQS

    @pl.when(step == 0)
    def _():
        q, f, v, gate = _hgrn_gates(x_ref[...].astype(BF16), win_ref,
                                    _lower_bound(lbraw_ref[...], layer))
        hi, mid, lo = _split3(f)
        f3_s[0] = hi
        f3_s[1] = mid
        f3_s[2] = lo
        k_s[...] = 1.0 - f
        q_s[...] = q
        v_s[...] = v
        gate_s[...] = gate

    rows = pl.ds(pl.multiple_of(step * nb, nb), nb)
    fhi = f3_s[0, rows, :]
    fmid = f3_s[1, rows, :]
    flo = f3_s[2, rows, :]
    k8 = k_s[rows, :]
    q8 = q_s[rows, :]
    v8 = v_s[rows, :]
    ones = jnp.ones((nb, DV_A), F32)
    zeros = jnp.zeros((nb, DV_A), F32)
    rhs = []
    for h in range(H_A):
        ls = slice(h * DK_A, (h + 1) * DK_A)
        left = jnp.concatenate([ones, ones, ones, zeros], axis=0)
        right = jnp.concatenate([zeros, zeros, zeros, v8[:, ls]], axis=0)
        rhs.append(jnp.concatenate([left, right], axis=1).astype(BF16))
    seq = lax.broadcasted_iota(jnp.int32, (nb, DK_A), 0)
    head_lanes = [slice(h * DK_A, (h + 1) * DK_A) for h in range(H_A)]

    def group_body(p, o8):
        seqs = [p * DEC_SEQ_GROUP + i for i in range(DEC_SEQ_GROUP)]
        sels = [seq == i for i in seqs]
        chains = [(si, h) for si in range(DEC_SEQ_GROUP) for h in range(H_A)]
        s_old = [s_ref[seqs[si], h] for si, h in chains]
        lhs = [jnp.concatenate([jnp.where(sels[si], a[:, head_lanes[h]], 0.0)
                                for a in (fhi, fmid, flo, k8)], axis=0).astype(BF16)
               for si, h in chains]
        fb_kv = [_dot_tn(lhs[c], rhs[h]) for c, (si, h) in enumerate(chains)]
        s_new = [fb_kv[c][:, :DV_A] * s_old[c] + fb_kv[c][:, DV_A:] for c in range(len(chains))]
        qsel = [jnp.where(sels[si], q8[:, head_lanes[h]], 0.0).astype(BF16) for si, h in chains]
        outs = [_dot(qsel[c], s_new[c].astype(BF16)) for c in range(len(chains))]
        for c, (si, h) in enumerate(chains):
            snew_ref[seqs[si], h] = s_new[c]
        for si in range(DEC_SEQ_GROUP):
            o8 = o8 + jnp.concatenate(outs[si * H_A:(si + 1) * H_A], axis=1)
        return o8

    oacc_s[rows, :] = lax.fori_loop(0, nb // DEC_SEQ_GROUP, group_body, jnp.zeros((nb, d), F32))

    @pl.when(step == pl.num_programs(1) - 1)
    def _():
        o = oacc_s[...]
        norm_g = ng_ref[...]
        heads = []
        for h in range(H_A):
            oh = o[:, h * DV_A:(h + 1) * DV_A]
            ms = jnp.mean(oh * oh, axis=-1, keepdims=True)
            heads.append(oh * lax.rsqrt(ms + RMS_EPS) * norm_g)
        on = jnp.concatenate(heads, axis=1) * gate_s[...]
        y = _dot(on.astype(BF16), wout_ref[...])
        o_ref[...] = _layer_norm(ALPHA * x_ref[...] + y, lng_ref[...], lnb_ref[...])


def _hgrn_decode(x, layer, lb_raw, w_in, norm_g, w_out, ln_g, ln_b, states, new_states):
    n, d = x.shape
    j = layer // N_MIXERS
    assert n % DEC_SEQS == 0 and DEC_SEQS % DEC_SEQ_GROUP == 0
    chained = new_states is not None
    steps = n // DEC_SEQS
    passes = 1 if chained else states.shape[0] - j
    row_scratch = pltpu.VMEM((n, d), F32)
    state_block = (None, DEC_SEQS, H_A, DK_A, DV_A)
    state_in = pl.BlockSpec(state_block,
                            lambda g, s: (j, jnp.where(g == 0, s, steps - 1), 0, 0, 0))
    state_out = pl.BlockSpec(state_block, lambda g, s: (j + g, s, 0, 0, 0))
    operands = [x, lb_raw, w_in, norm_g, w_out, ln_g, ln_b, states]
    in_specs = [
        _const_spec(x.shape),
        _const_spec(lb_raw.shape),
        _layer_spec(w_in.shape, j),
        _layer_spec(norm_g.shape, j),
        _layer_spec(w_out.shape, j),
        _layer_spec(ln_g.shape, layer),
        _layer_spec(ln_b.shape, layer),
        state_in,
    ]
    if chained:
        operands.append(new_states)
        in_specs.append(pl.BlockSpec(memory_space=pl.ANY))
    return pl.pallas_call(
        functools.partial(_hgrn_decode_kernel, layer, chained),
        grid=(passes, steps),
        in_specs=in_specs,
        out_specs=[pl.BlockSpec((n, d), lambda g, s: (0, 0)), state_out],
        out_shape=[
            jax.ShapeDtypeStruct((n, d), F32),
            jax.ShapeDtypeStruct(states.shape, F32),
        ],
        scratch_shapes=[
            pltpu.VMEM((3, n, d), F32),
            row_scratch, row_scratch, row_scratch, row_scratch, row_scratch,
        ],
        input_output_aliases={len(operands) - 1: 1} if chained else {},
        compiler_params=_params(2),
        name="hgrn_decode",
    )(*operands)


def _cmlp_gate_values(x, win_ref, vg_ref, vb_ref):
    xb = x.astype(BF16)
    di = D_INNER_B
    u = _gelu(_dot(xb, win_ref[:, 0:di]))
    v = _gelu(_dot(xb, win_ref[:, di:2 * di]))
    return u, _layer_norm(v, vg_ref[...], vb_ref[...])


def _cmlp_finish(x, u, mixed, wout_ref, lng_ref, lnb_ref):
    y = _dot((u * mixed).astype(BF16), wout_ref[...])
    return _layer_norm(ALPHA * x + y, lng_ref[...], lnb_ref[...])


def _cmlp_kernel(x_ref, xdec_ref, win_ref, vg_ref, vb_ref, ws_ref, bias_ref, wout_ref,
                 lng_ref, lnb_ref, o_ref, vrows_ref, odec_ref, vdec_ref):
    x = x_ref[...]
    n = x.shape[0]
    u, v = _cmlp_gate_values(x, win_ref, vg_ref, vb_ref)
    row = lax.broadcasted_iota(jnp.int32, (CHUNK_B, CHUNK_B), 0)
    col = lax.broadcasted_iota(jnp.int32, (CHUNK_B, CHUNK_B), 1)
    w = [jnp.where(row >= col, ws_ref[g], 0.0).astype(BF16) for g in range(G_B)]
    bias = bias_ref[...]
    vb16 = v.astype(BF16)
    chunks = []
    for c in range(n // CHUNK_B):
        rs = slice(c * CHUNK_B, (c + 1) * CHUNK_B)
        groups = [_dot(w[g], vb16[rs, g * DG_B:(g + 1) * DG_B]) for g in range(G_B)]
        chunks.append(jnp.concatenate(groups, axis=1) + bias)
    o_ref[...] = _cmlp_finish(x, u, jnp.concatenate(chunks, axis=0), wout_ref, lng_ref, lnb_ref)

    @pl.when(pl.program_id(1) == pl.num_programs(1) - 1)
    def _():
        vrows_ref[...] = v[n - CHUNK_B:, :]

    @pl.when((pl.program_id(0) == 0) & (pl.program_id(1) == 0))
    def _():
        xd = xdec_ref[...]
        ud, vd = _cmlp_gate_values(xd, win_ref, vg_ref, vb_ref)
        w00 = jnp.concatenate([jnp.broadcast_to(ws_ref[g, 0:1, 0:1], (1, DG_B))
                               for g in range(G_B)], axis=1)
        vdec_ref[...] = vd
        odec_ref[...] = _cmlp_finish(xd, ud, vd * w00 + bias[0:1, :], wout_ref, lng_ref, lnb_ref)


def _cmlp(x, x_dec, j, layer, w_in, v_g, v_b, w_s, bias_full, w_out, ln_g, ln_b):
    b, l, d = x.shape
    assert l % ROW_TILE == 0 and ROW_TILE % CHUNK_B == 0
    dec_out = pl.BlockSpec(x_dec.shape, lambda i, t: (0, 0))
    return pl.pallas_call(
        _cmlp_kernel,
        grid=(b, l // ROW_TILE),
        in_specs=[
            pl.BlockSpec((None, ROW_TILE, d), lambda i, t: (i, t, 0)),
            _const_spec(x_dec.shape),
            _layer_spec(w_in.shape, j),
            _layer_spec(v_g.shape, j),
            _layer_spec(v_b.shape, j),
            _layer_spec(w_s.shape, j),
            _layer_spec(bias_full.shape, j),
            _layer_spec(w_out.shape, j),
            _layer_spec(ln_g.shape, layer),
            _layer_spec(ln_b.shape, layer),
        ],
        out_specs=[
            pl.BlockSpec((None, ROW_TILE, d), lambda i, t: (i, t, 0)),
            pl.BlockSpec((None, CHUNK_B, D_INNER_B), lambda i, t: (i, 0, 0)),
            dec_out,
            dec_out,
        ],
        out_shape=[
            jax.ShapeDtypeStruct((b, l, d), F32),
            jax.ShapeDtypeStruct((b, CHUNK_B, D_INNER_B), F32),
            jax.ShapeDtypeStruct(x_dec.shape, F32),
            jax.ShapeDtypeStruct(x_dec.shape, F32),
        ],
        compiler_params=_params(2),
        name="cmlp",
    )(x, x_dec, w_in, v_g, v_b, w_s, bias_full, w_out, ln_g, ln_b)


def _ffn_rows(x, win_ref, wout_ref, ln_g, ln_b):
    xb = x.astype(BF16)
    acc = jnp.zeros(x.shape, F32)
    for j in range(D_FF // FF_TILE):
        cs = slice(j * FF_TILE, (j + 1) * FF_TILE)
        us = slice(D_FF + j * FF_TILE, D_FF + (j + 1) * FF_TILE)
        hcol = jax.nn.silu(_dot(xb, win_ref[:, cs])) * _dot(xb, win_ref[:, us])
        acc = acc + _dot(hcol.astype(BF16), wout_ref[cs, :])
    return _layer_norm(ALPHA * x + acc, ln_g, ln_b)


def _ffn_kernel(x_ref, xdec_ref, win_ref, wout_ref, lng_ref, lnb_ref, o_ref, odec_ref):
    ln_g, ln_b = lng_ref[...], lnb_ref[...]
    o_ref[...] = _ffn_rows(x_ref[...], win_ref, wout_ref, ln_g, ln_b)

    @pl.when(pl.program_id(0) == 0)
    def _():
        odec_ref[...] = _ffn_rows(xdec_ref[...], win_ref, wout_ref, ln_g, ln_b)


def _ffn(x, x_dec, layer, w_in, w_out, ln_g, ln_b):
    n, d = x.shape
    assert n % FFN_ROW_TILE == 0 and D_FF % FF_TILE == 0
    return pl.pallas_call(
        _ffn_kernel,
        grid=(n // FFN_ROW_TILE,),
        in_specs=[
            pl.BlockSpec((FFN_ROW_TILE, d), lambda i: (i, 0)),
            _const_spec(x_dec.shape),
            _layer_spec(w_in.shape, layer),
            _layer_spec(w_out.shape, layer),
            _layer_spec(ln_g.shape, layer),
            _layer_spec(ln_b.shape, layer),
        ],
        out_specs=[
            pl.BlockSpec((FFN_ROW_TILE, d), lambda i: (i, 0)),
            pl.BlockSpec(x_dec.shape, lambda i: (0, 0)),
        ],
        out_shape=[
            jax.ShapeDtypeStruct((n, d), F32),
            jax.ShapeDtypeStruct(x_dec.shape, F32),
        ],
        compiler_params=_params(1),
        name="ffn",
    )(x, x_dec, w_in, w_out, ln_g, ln_b)


def kernel(x_prompt, x_sample, state_hgrn, ln_mix_g, ln_mix_b, ln_ffn_g, ln_ffn_b, a_lb_raw, a_w_in, a_norm_g, a_w_out, b_w_in, b_ln_g, b_ln_b, b_w_s, b_bias_s, b_w_out, ffn_w_in, ffn_w_out):
    bsz, seq, d = x_prompt.shape
    n_dec = x_sample.shape[0]
    assert x_sample.shape[1] == 1 and seq % CHUNK_B == 0

    a_w_in_b = a_w_in.astype(BF16)
    a_w_out_b = a_w_out.astype(BF16)
    b_w_in_b = b_w_in.astype(BF16)
    b_w_out_b = b_w_out.astype(BF16)
    ffn_w_in_b = ffn_w_in.astype(BF16)
    ffn_w_out_b = ffn_w_out.astype(BF16)
    masks = jnp.asarray(_level_masks(CHUNK_A))
    signs = jnp.asarray(_level_signs(CHUNK_A, DK_A))
    lb_raw = a_lb_raw.astype(F32)
    bias_full = jnp.repeat(jnp.swapaxes(b_bias_s, 1, 2), DG_B, axis=2)

    def rows(p):
        return p[:, None, :]

    mix_g, mix_b, ffn_g, ffn_b = rows(ln_mix_g), rows(ln_mix_b), rows(ln_ffn_g), rows(ln_ffn_b)
    norm_g, v_g, v_b = rows(a_norm_g), rows(b_ln_g), rows(b_ln_b)

    xp = x_prompt
    xs = x_sample.reshape(n_dec, d)
    st_p, v_p, v_s = [], [], []
    st_s = None
    for layer in range(DEPTH):
        j = layer // N_MIXERS
        if layer % N_MIXERS == 0:
            xp, s = _hgrn_prompt(xp, layer, lb_raw, a_w_in_b, norm_g, a_w_out_b, mix_g, mix_b,
                                 masks, signs)
            st_p.append(s)
            xs, st_s = _hgrn_decode(xs, layer, lb_raw, a_w_in_b, norm_g, a_w_out_b, mix_g, mix_b,
                                    state_hgrn, st_s)
        else:
            xp, v, xs, vd = _cmlp(xp, xs, j, layer, b_w_in_b, v_g, v_b, b_w_s, bias_full, b_w_out_b,
                                  mix_g, mix_b)
            v_p.append(v)
            v_s.append(vd[:, None, :])
        xp, xs = _ffn(xp.reshape(bsz * seq, d), xs, layer, ffn_w_in_b, ffn_w_out_b, ffn_g, ffn_b)
        xp = xp.reshape(bsz, seq, d)

    return (xp, xs.reshape(n_dec, 1, d), jnp.stack(st_p), st_s, jnp.stack(v_p), jnp.stack(v_s))
```

```python
import functools

import numpy as np
import jax
import jax.numpy as jnp
from jax import lax
from jax.experimental import pallas as pl
from jax.experimental.pallas import tpu as pltpu

D_MODEL = 1024
DEPTH = 4
N_MIXERS = 2
DK_A = 128
H_A = D_MODEL // DK_A
DV_A = D_MODEL // H_A
D_INNER_B = D_MODEL
CHUNK_B = 128
G_B = 8
DG_B = D_INNER_B // G_B
D_FF = -(-(8 * D_MODEL) // (3 * 256)) * 256
ALPHA = (2 * DEPTH) ** 0.25
LN_EPS = 1e-5
RMS_EPS = 1e-6

LANES = 128
SUBLANES = 8
MXU_DIM = 256
VMEM_LIMIT_BYTES = 56 * 2**20

ROW_TILE = 512
FFN_ROW_TILE = 1024
FFN_ROW_SPLIT = 2
CHUNK_A = 128
HEAD_GROUP = 8
SUB_A = 32
DECAY_LIMIT_LOG2 = 96.0
FF_TILE = MXU_DIM
DEC_SEQS = 8
DEC_SEQ_GROUP = 2

BF16 = jnp.bfloat16
F32 = jnp.float32


def _level_halves(chunk):
    out, h = [], chunk // 2
    while h >= 1:
        out.append(h)
        h //= 2
    return tuple(out)


LEVEL_HALVES = _level_halves(CHUNK_A)


def _level_masks(chunk):
    t = np.arange(chunk)[:, None]
    s = np.arange(chunk)[None, :]
    masks = [((t // (2 * h) == s // (2 * h)) & (t % (2 * h) >= h) & (s % (2 * h) < h))
             for h in _level_halves(chunk)]
    masks.append(t == s)
    masks.append((t // SUB_A == s // SUB_A) & (t >= s))
    return np.stack(masks).astype(np.float32)


def _level_signs(chunk, width):
    t = np.arange(chunk)[:, None]
    signs = [np.where(t % (2 * h) >= h, 1.0, -1.0) for h in _level_halves(chunk)]
    return np.broadcast_to(np.stack(signs), (len(signs), chunk, width)).astype(np.float32)


def _dot(a, b):
    return jnp.dot(a, b, preferred_element_type=F32)


def _dot_nt(a, b):
    return lax.dot_general(a, b, (((1,), (1,)), ((), ())), preferred_element_type=F32)


def _dot_tn(a, b):
    return lax.dot_general(a, b, (((0,), (0,)), ((), ())), preferred_element_type=F32)


def _layer_norm(z, g, b):
    mu = jnp.mean(z, axis=-1, keepdims=True)
    zc = z - mu
    var = jnp.mean(zc * zc, axis=-1, keepdims=True)
    return zc * lax.rsqrt(var + LN_EPS) * g + b


def _gelu(x):
    return 0.5 * x * (1.0 + lax.erf(x * (2.0 ** -0.5)))


def _lower_bound(raw, layer):
    m = jnp.max(raw, axis=0, keepdims=True)
    e = jnp.exp(raw - m)
    p = e / jnp.sum(e, axis=0, keepdims=True)
    c = p[0:1]
    for i in range(1, layer + 1):
        c = c + p[i:i + 1]
    return c - p[0:1]


def _hgrn_gates(xb, win_ref, lb):
    d = D_MODEL
    q = jax.nn.silu(_dot(xb, win_ref[:, 0:d])) * (DK_A ** -0.5)
    f = lb + (1.0 - lb) * jax.nn.sigmoid(_dot(xb, win_ref[:, d:2 * d]))
    v = _dot(xb, win_ref[:, 2 * d:3 * d])
    gate = jax.nn.silu(_dot(xb, win_ref[:, 3 * d:4 * d]))
    return q, f, v, gate


def _split2(a):
    hi = a.astype(BF16)
    lo = (a - hi.astype(F32)).astype(BF16)
    return jnp.concatenate([hi, lo], axis=1)


def _level_ref(g, half):
    n, w = g.shape
    if half >= SUBLANES // 2:
        blk = 2 * half
        parts = [jnp.broadcast_to(g[b * blk + half - 1:b * blk + half, :], (blk, w))
                 for b in range(n // blk)]
        return parts[0] if len(parts) == 1 else jnp.concatenate(parts, axis=0)
    row = lax.broadcasted_iota(jnp.int32, g.shape, 0)
    if half == 2:
        m = row % 4
        return jnp.where(m == 0, pltpu.roll(g, n - 1, axis=0),
                         jnp.where(m == 1, g,
                                   jnp.where(m == 2, pltpu.roll(g, 1, axis=0),
                                             pltpu.roll(g, 2, axis=0))))
    assert half == 1
    return jnp.where(row % 2 == 0, g, pltpu.roll(g, 1, axis=0))


def _sub_block_ref(g):
    n, w = g.shape
    parts = [jnp.zeros((SUB_A, w), g.dtype)]
    parts += [jnp.broadcast_to(g[b * SUB_A - 1:b * SUB_A, :], (SUB_A, w))
              for b in range(1, n // SUB_A)]
    return jnp.concatenate(parts, axis=0)


def _hgrn_group_chunk(stable, heads, c, tri, q_s, k_s, v_s, lf_s, oh_s, st_ref, mask_ref, sign_ref,
                      norm_g):
    if isinstance(c, int):
        rows = slice(c * CHUNK_A, (c + 1) * CHUNK_A)
    else:
        rows = pl.ds(pl.multiple_of(c * CHUNK_A, CHUNK_A), CHUNK_A)
    n = range(len(heads))
    q = [q_s[h, rows, :] for h in heads]
    k = [k_s[h, rows, :] for h in heads]
    vb = [v_s[h, rows, :] for h in heads]
    st = [st_ref[h] for h in heads]
    g2 = [_dot(tri, lf_s[h, rows, :]) for h in heads]
    g = [a[:, :DK_A] + a[:, DK_A:] for a in g2]
    g_last = [a[CHUNK_A - 1:CHUNK_A, :] for a in g]
    o = [_dot_nt(q[i] * jnp.exp2(g[i]).astype(BF16), st[i].astype(BF16)) for i in n]
    ks = [k[i] * jnp.exp2(g_last[i] - g[i]).astype(BF16) for i in n]
    st_new = [st[i] * jnp.exp2(g_last[i]) + _dot_tn(vb[i], ks[i]) for i in n]
    if stable:
        levels = [(lvl, half) for lvl, half in enumerate(LEVEL_HALVES) if 2 * half > SUB_A]
        near = mask_ref[len(LEVEL_HALVES) + 1]
        d = [g[i] - _sub_block_ref(g[i]) for i in n]
        a = [near * _dot_nt(q[i] * jnp.exp2(d[i]).astype(BF16),
                            k[i] * jnp.exp2(-d[i]).astype(BF16)) for i in n]
    else:
        levels = list(enumerate(LEVEL_HALVES))
        diag = mask_ref[len(LEVEL_HALVES)]
        a = [diag * _dot_nt(q[i], k[i]) for i in n]
    for lvl, half in levels:
        e = [jnp.exp2((g[i] - _level_ref(g[i], half)) * sign_ref[lvl]).astype(BF16) for i in n]
        p = [_dot_nt(q[i] * e[i], k[i] * e[i]) for i in n]
        a = [a[i] + mask_ref[lvl] * p[i] for i in n]
    o = [o[i] + _dot(a[i].astype(BF16), vb[i]) for i in n]
    ms = [jnp.mean(o[i] * o[i], axis=-1, keepdims=True) for i in n]
    for i, h in enumerate(heads):
        st_ref[h] = st_new[i]
        oh_s[h, rows, :] = o[i] * lax.rsqrt(ms[i] + RMS_EPS) * norm_g


def _hgrn_prompt_kernel(layer, x_ref, lbraw_ref, win_ref, ng_ref, wout_ref, lng_ref, lnb_ref,
                        mask_ref, sign_ref, o_ref, sfin_ref, st_ref, q_s, k_s, v_s, lf_s, oh_s,
                        gate_s):
    t = pl.program_id(1)

    @pl.when(t == 0)
    def _():
        st_ref[...] = jnp.zeros_like(st_ref)

    x = x_ref[0]
    q, f, v, gate = _hgrn_gates(x.astype(BF16), win_ref, _lower_bound(lbraw_ref[...], layer))
    k = 1.0 - f
    lf = jnp.log2(f)
    gate_s[...] = gate
    for h in range(H_A):
        ls = slice(h * DK_A, (h + 1) * DK_A)
        q_s[h] = q[:, ls].astype(BF16)
        k_s[h] = k[:, ls].astype(BF16)
        v_s[h] = v[:, ls].astype(BF16)
        lf_s[h] = _split2(lf[:, ls])

    norm_g = ng_ref[...]
    row = lax.broadcasted_iota(jnp.int32, (CHUNK_A, CHUNK_A), 0)
    col = lax.broadcasted_iota(jnp.int32, (CHUNK_A, CHUNK_A), 1)
    tri = (row >= col).astype(BF16)

    sub_decay = jnp.sum(lf.reshape(ROW_TILE // SUB_A, SUB_A, D_MODEL), axis=1)
    stable = jnp.min(sub_decay) >= -DECAY_LIMIT_LOG2

    def finish_rows(rows):
        o = jnp.concatenate([oh_s[h, rows, :] for h in range(H_A)], axis=1) * gate_s[rows, :]
        y = _dot(o.astype(BF16), wout_ref[...])
        o_ref[0, rows, :] = _layer_norm(ALPHA * x_ref[0, rows, :] + y, lng_ref[...], lnb_ref[...])

    def run_stable():
        heads = list(range(H_A))
        for c in range(ROW_TILE // CHUNK_A):
            _hgrn_group_chunk(True, heads, c, tri, q_s, k_s, v_s, lf_s, oh_s, st_ref, mask_ref,
                              sign_ref, norm_g)
            if c > 0:
                finish_rows(slice((c - 1) * CHUNK_A, c * CHUNK_A))
        finish_rows(slice(ROW_TILE - CHUNK_A, ROW_TILE))

    def run_general():
        def group_body(hg, carry):
            heads = [hg * HEAD_GROUP + i for i in range(HEAD_GROUP)]

            def chunk_body(c, carry2):
                _hgrn_group_chunk(False, heads, c, tri, q_s, k_s, v_s, lf_s, oh_s, st_ref,
                                  mask_ref, sign_ref, norm_g)
                return carry2
            return lax.fori_loop(0, ROW_TILE // CHUNK_A, chunk_body, carry)

        lax.fori_loop(0, H_A // HEAD_GROUP, group_body, 0)
        finish_rows(slice(0, ROW_TILE))

    lax.cond(stable, run_stable, run_general)

    @pl.when(t == pl.num_programs(1) - 1)
    def _():
        for h in range(H_A):
            sfin_ref[0, h] = st_ref[h].T


def _const_spec(shape):
    nd = len(shape)
    return pl.BlockSpec(shape, lambda *_: (0,) * nd, pipeline_mode=pl.Buffered(1))


def _layer_spec(shape, j):
    nd = len(shape)
    return pl.BlockSpec((None,) + tuple(shape[1:]), lambda *_: (j,) + (0,) * (nd - 1),
                        pipeline_mode=pl.Buffered(1))


def _params(n_axes):
    return pltpu.CompilerParams(dimension_semantics=("arbitrary",) * n_axes,
                                vmem_limit_bytes=VMEM_LIMIT_BYTES)


def _hgrn_prompt(x, layer, lb_raw, w_in, norm_g, w_out, ln_g, ln_b, masks, signs):
    b, l, d = x.shape
    j = layer // N_MIXERS
    assert l % ROW_TILE == 0 and ROW_TILE % CHUNK_A == 0 and H_A % HEAD_GROUP == 0
    head_operand = pltpu.VMEM((H_A, ROW_TILE, DK_A), BF16)
    return pl.pallas_call(
        functools.partial(_hgrn_prompt_kernel, layer),
        grid=(b, l // ROW_TILE),
        in_specs=[
            pl.BlockSpec((1, ROW_TILE, d), lambda i, t: (i, t, 0)),
            _const_spec(lb_raw.shape),
            _layer_spec(w_in.shape, j),
            _layer_spec(norm_g.shape, j),
            _layer_spec(w_out.shape, j),
            _layer_spec(ln_g.shape, layer),
            _layer_spec(ln_b.shape, layer),
            _const_spec(masks.shape),
            _const_spec(signs.shape),
        ],
        out_specs=[
            pl.BlockSpec((1, ROW_TILE, d), lambda i, t: (i, t, 0)),
            pl.BlockSpec((1, H_A, DK_A, DV_A), lambda i, t: (i, 0, 0, 0)),
        ],
        out_shape=[
            jax.ShapeDtypeStruct((b, l, d), F32),
            jax.ShapeDtypeStruct((b, H_A, DK_A, DV_A), F32),
        ],
        scratch_shapes=[
            pltpu.VMEM((H_A, DV_A, DK_A), F32),
            head_operand, head_operand, head_operand,
            pltpu.VMEM((H_A, ROW_TILE, 2 * DK_A), BF16),
            pltpu.VMEM((H_A, ROW_TILE, DV_A), F32),
            pltpu.VMEM((ROW_TILE, d), F32),
        ],
        compiler_params=_params(2),
        name="hgrn_prompt",
    )(x, lb_raw, w_in, norm_g, w_out, ln_g, ln_b, masks, signs)


def _split3(f):
    hi = f.astype(BF16).astype(F32)
    r = f - hi
    mid = r.astype(BF16).astype(F32)
    lo = (r - mid).astype(BF16).astype(F32)
    return hi, mid, lo


def _hgrn_decode_kernel(layer, chained, x_ref, lbraw_ref, win_ref, ng_ref, wout_ref, lng_ref,
                        lnb_ref, s_ref, *rest):
    o_ref, snew_ref, f3_s, k_s, q_s, v_s, gate_s, oacc_s = rest[1:] if chained else rest

    @pl.when(pl.program_id(0) == 0)
    def _():
        _hgrn_decode_step(layer, x_ref, lbraw_ref, win_ref, ng_ref, wout_ref, lng_ref, lnb_ref,
                          s_ref, o_ref, snew_ref, f3_s, k_s, q_s, v_s, gate_s, oacc_s)

    @pl.when(pl.program_id(0) > 0)
    def _():
        snew_ref[...] = jnp.zeros_like(snew_ref)


def _hgrn_decode_step(layer, x_ref, lbraw_ref, win_ref, ng_ref, wout_ref, lng_ref, lnb_ref,
                      s_ref, o_ref, snew_ref, f3_s, k_s, q_s, v_s, gate_s, oacc_s):
    step = pl.program_id(1)
    d = D_MODEL
    nb = DEC_SEQS

    @pl.when(step == 0)
    def _():
        q, f, v, gate = _hgrn_gates(x_ref[...].astype(BF16), win_ref,
                                    _lower_bound(lbraw_ref[...], layer))
        hi, mid, lo = _split3(f)
        f3_s[0] = hi
        f3_s[1] = mid
        f3_s[2] = lo
        k_s[...] = 1.0 - f
        q_s[...] = q
        v_s[...] = v
        gate_s[...] = gate

    rows = pl.ds(pl.multiple_of(step * nb, nb), nb)
    fhi = f3_s[0, rows, :]
    fmid = f3_s[1, rows, :]
    flo = f3_s[2, rows, :]
    k8 = k_s[rows, :]
    q8 = q_s[rows, :]
    v8 = v_s[rows, :]
    ones = jnp.ones((nb, DV_A), F32)
    zeros = jnp.zeros((nb, DV_A), F32)
    rhs = []
    for h in range(H_A):
        ls = slice(h * DK_A, (h + 1) * DK_A)
        left = jnp.concatenate([ones, ones, ones, zeros], axis=0)
        right = jnp.concatenate([zeros, zeros, zeros, v8[:, ls]], axis=0)
        rhs.append(jnp.concatenate([left, right], axis=1).astype(BF16))
    seq = lax.broadcasted_iota(jnp.int32, (nb, DK_A), 0)
    head_lanes = [slice(h * DK_A, (h + 1) * DK_A) for h in range(H_A)]

    def group_body(p, o8):
        seqs = [p * DEC_SEQ_GROUP + i for i in range(DEC_SEQ_GROUP)]
        sels = [seq == i for i in seqs]
        chains = [(si, h) for si in range(DEC_SEQ_GROUP) for h in range(H_A)]
        s_old = [s_ref[seqs[si], h] for si, h in chains]
        lhs = [jnp.concatenate([jnp.where(sels[si], a[:, head_lanes[h]], 0.0)
                                for a in (fhi, fmid, flo, k8)], axis=0).astype(BF16)
               for si, h in chains]
        fb_kv = [_dot_tn(lhs[c], rhs[h]) for c, (si, h) in enumerate(chains)]
        s_new = [fb_kv[c][:, :DV_A] * s_old[c] + fb_kv[c][:, DV_A:] for c in range(len(chains))]
        qsel = [jnp.where(sels[si], q8[:, head_lanes[h]], 0.0).astype(BF16) for si, h in chains]
        outs = [_dot(qsel[c], s_new[c].astype(BF16)) for c in range(len(chains))]
        for c, (si, h) in enumerate(chains):
            snew_ref[seqs[si], h] = s_new[c]
        for si in range(DEC_SEQ_GROUP):
            o8 = o8 + jnp.concatenate(outs[si * H_A:(si + 1) * H_A], axis=1)
        return o8

    oacc_s[rows, :] = lax.fori_loop(0, nb // DEC_SEQ_GROUP, group_body, jnp.zeros((nb, d), F32))

    @pl.when(step == pl.num_programs(1) - 1)
    def _():
        o = oacc_s[...]
        norm_g = ng_ref[...]
        heads = []
        for h in range(H_A):
            oh = o[:, h * DV_A:(h + 1) * DV_A]
            ms = jnp.mean(oh * oh, axis=-1, keepdims=True)
            heads.append(oh * lax.rsqrt(ms + RMS_EPS) * norm_g)
        on = jnp.concatenate(heads, axis=1) * gate_s[...]
        y = _dot(on.astype(BF16), wout_ref[...])
        o_ref[...] = _layer_norm(ALPHA * x_ref[...] + y, lng_ref[...], lnb_ref[...])


def _hgrn_decode(x, layer, lb_raw, w_in, norm_g, w_out, ln_g, ln_b, states, new_states):
    n, d = x.shape
    j = layer // N_MIXERS
    assert n % DEC_SEQS == 0 and DEC_SEQS % DEC_SEQ_GROUP == 0
    chained = new_states is not None
    steps = n // DEC_SEQS
    passes = 1 if chained else states.shape[0] - j
    row_scratch = pltpu.VMEM((n, d), F32)
    state_block = (None, DEC_SEQS, H_A, DK_A, DV_A)
    state_in = pl.BlockSpec(state_block,
                            lambda g, s: (j, jnp.where(g == 0, s, steps - 1), 0, 0, 0))
    state_out = pl.BlockSpec(state_block, lambda g, s: (j + g, s, 0, 0, 0))
    operands = [x, lb_raw, w_in, norm_g, w_out, ln_g, ln_b, states]
    in_specs = [
        _const_spec(x.shape),
        _const_spec(lb_raw.shape),
        _layer_spec(w_in.shape, j),
        _layer_spec(norm_g.shape, j),
        _layer_spec(w_out.shape, j),
        _layer_spec(ln_g.shape, layer),
        _layer_spec(ln_b.shape, layer),
        state_in,
    ]
    if chained:
        operands.append(new_states)
        in_specs.append(pl.BlockSpec(memory_space=pl.ANY))
    return pl.pallas_call(
        functools.partial(_hgrn_decode_kernel, layer, chained),
        grid=(passes, steps),
        in_specs=in_specs,
        out_specs=[pl.BlockSpec((n, d), lambda g, s: (0, 0)), state_out],
        out_shape=[
            jax.ShapeDtypeStruct((n, d), F32),
            jax.ShapeDtypeStruct(states.shape, F32),
        ],
        scratch_shapes=[
            pltpu.VMEM((3, n, d), F32),
            row_scratch, row_scratch, row_scratch, row_scratch, row_scratch,
        ],
        input_output_aliases={len(operands) - 1: 1} if chained else {},
        compiler_params=_params(2),
        name="hgrn_decode",
    )(*operands)


def _cmlp_gate_values(x, win_ref, vg_ref, vb_ref):
    xb = x.astype(BF16)
    di = D_INNER_B
    u = _gelu(_dot(xb, win_ref[:, 0:di]))
    v = _gelu(_dot(xb, win_ref[:, di:2 * di]))
    return u, _layer_norm(v, vg_ref[...], vb_ref[...])


def _cmlp_finish(x, u, mixed, wout_ref, lng_ref, lnb_ref):
    y = _dot((u * mixed).astype(BF16), wout_ref[...])
    return _layer_norm(ALPHA * x + y, lng_ref[...], lnb_ref[...])


def _cmlp_kernel(x_ref, xdec_ref, win_ref, vg_ref, vb_ref, ws_ref, bias_ref, wout_ref,
                 lng_ref, lnb_ref, o_ref, vrows_ref, odec_ref, vdec_ref):
    x = x_ref[...]
    n = x.shape[0]
    u, v = _cmlp_gate_values(x, win_ref, vg_ref, vb_ref)
    row = lax.broadcasted_iota(jnp.int32, (CHUNK_B, CHUNK_B), 0)
    col = lax.broadcasted_iota(jnp.int32, (CHUNK_B, CHUNK_B), 1)
    w = [jnp.where(row >= col, ws_ref[g], 0.0).astype(BF16) for g in range(G_B)]
    bias = bias_ref[...]
    vb16 = v.astype(BF16)
    chunks = []
    for c in range(n // CHUNK_B):
        rs = slice(c * CHUNK_B, (c + 1) * CHUNK_B)
        groups = [_dot(w[g], vb16[rs, g * DG_B:(g + 1) * DG_B]) for g in range(G_B)]
        chunks.append(jnp.concatenate(groups, axis=1) + bias)
    o_ref[...] = _cmlp_finish(x, u, jnp.concatenate(chunks, axis=0), wout_ref, lng_ref, lnb_ref)

    @pl.when(pl.program_id(1) == pl.num_programs(1) - 1)
    def _():
        vrows_ref[...] = v[n - CHUNK_B:, :]

    @pl.when((pl.program_id(0) == 0) & (pl.program_id(1) == 0))
    def _():
        xd = xdec_ref[...]
        ud, vd = _cmlp_gate_values(xd, win_ref, vg_ref, vb_ref)
        w00 = jnp.concatenate([jnp.broadcast_to(ws_ref[g, 0:1, 0:1], (1, DG_B))
                               for g in range(G_B)], axis=1)
        vdec_ref[...] = vd
        odec_ref[...] = _cmlp_finish(xd, ud, vd * w00 + bias[0:1, :], wout_ref, lng_ref, lnb_ref)


def _cmlp(x, x_dec, j, layer, w_in, v_g, v_b, w_s, bias_full, w_out, ln_g, ln_b):
    b, l, d = x.shape
    assert l % ROW_TILE == 0 and ROW_TILE % CHUNK_B == 0
    dec_out = pl.BlockSpec(x_dec.shape, lambda i, t: (0, 0))
    return pl.pallas_call(
        _cmlp_kernel,
        grid=(b, l // ROW_TILE),
        in_specs=[
            pl.BlockSpec((None, ROW_TILE, d), lambda i, t: (i, t, 0)),
            _const_spec(x_dec.shape),
            _layer_spec(w_in.shape, j),
            _layer_spec(v_g.shape, j),
            _layer_spec(v_b.shape, j),
            _layer_spec(w_s.shape, j),
            _layer_spec(bias_full.shape, j),
            _layer_spec(w_out.shape, j),
            _layer_spec(ln_g.shape, layer),
            _layer_spec(ln_b.shape, layer),
        ],
        out_specs=[
            pl.BlockSpec((None, ROW_TILE, d), lambda i, t: (i, t, 0)),
            pl.BlockSpec((None, CHUNK_B, D_INNER_B), lambda i, t: (i, 0, 0)),
            dec_out,
            dec_out,
        ],
        out_shape=[
            jax.ShapeDtypeStruct((b, l, d), F32),
            jax.ShapeDtypeStruct((b, CHUNK_B, D_INNER_B), F32),
            jax.ShapeDtypeStruct(x_dec.shape, F32),
            jax.ShapeDtypeStruct(x_dec.shape, F32),
        ],
        compiler_params=_params(2),
        name="cmlp",
    )(x, x_dec, w_in, v_g, v_b, w_s, bias_full, w_out, ln_g, ln_b)


def _ffn_rows(x, win_ref, wout_ref, ln_g, ln_b):
    xb = x.astype(BF16)
    acc = jnp.zeros(x.shape, F32)
    for j in range(D_FF // FF_TILE):
        cs = slice(j * FF_TILE, (j + 1) * FF_TILE)
        us = slice(D_FF + j * FF_TILE, D_FF + (j + 1) * FF_TILE)
        hcol = jax.nn.silu(_dot(xb, win_ref[:, cs])) * _dot(xb, win_ref[:, us])
        acc = acc + _dot(hcol.astype(BF16), wout_ref[cs, :])
    return _layer_norm(ALPHA * x + acc, ln_g, ln_b)


def _ffn_kernel(x_ref, xdec_ref, win_ref, wout_ref, lng_ref, lnb_ref, o_ref, odec_ref):
    ln_g, ln_b = lng_ref[...], lnb_ref[...]
    half = x_ref.shape[0] // FFN_ROW_SPLIT
    for i in range(FFN_ROW_SPLIT):
        rs = slice(i * half, (i + 1) * half)
        o_ref[rs, :] = _ffn_rows(x_ref[rs, :], win_ref, wout_ref, ln_g, ln_b)

    @pl.when(pl.program_id(0) == 0)
    def _():
        odec_ref[...] = _ffn_rows(xdec_ref[...], win_ref, wout_ref, ln_g, ln_b)


def _ffn(x, x_dec, layer, w_in, w_out, ln_g, ln_b):
    n, d = x.shape
    assert n % FFN_ROW_TILE == 0 and D_FF % FF_TILE == 0
    return pl.pallas_call(
        _ffn_kernel,
        grid=(n // FFN_ROW_TILE,),
        in_specs=[
            pl.BlockSpec((FFN_ROW_TILE, d), lambda i: (i, 0)),
            _const_spec(x_dec.shape),
            _layer_spec(w_in.shape, layer),
            _layer_spec(w_out.shape, layer),
            _layer_spec(ln_g.shape, layer),
            _layer_spec(ln_b.shape, layer),
        ],
        out_specs=[
            pl.BlockSpec((FFN_ROW_TILE, d), lambda i: (i, 0)),
            pl.BlockSpec(x_dec.shape, lambda i: (0, 0)),
        ],
        out_shape=[
            jax.ShapeDtypeStruct((n, d), F32),
            jax.ShapeDtypeStruct(x_dec.shape, F32),
        ],
        compiler_params=_params(1),
        name="ffn",
    )(x, x_dec, w_in, w_out, ln_g, ln_b)


def kernel(x_prompt, x_sample, state_hgrn, ln_mix_g, ln_mix_b, ln_ffn_g, ln_ffn_b, a_lb_raw, a_w_in, a_norm_g, a_w_out, b_w_in, b_ln_g, b_ln_b, b_w_s, b_bias_s, b_w_out, ffn_w_in, ffn_w_out):
    bsz, seq, d = x_prompt.shape
    n_dec = x_sample.shape[0]
    assert x_sample.shape[1] == 1 and seq % CHUNK_B == 0

    a_w_in_b = a_w_in.astype(BF16)
    a_w_out_b = a_w_out.astype(BF16)
    b_w_in_b = b_w_in.astype(BF16)
    b_w_out_b = b_w_out.astype(BF16)
    ffn_w_in_b = ffn_w_in.astype(BF16)
    ffn_w_out_b = ffn_w_out.astype(BF16)
    masks = jnp.asarray(_level_masks(CHUNK_A))
    signs = jnp.asarray(_level_signs(CHUNK_A, DK_A))
    lb_raw = a_lb_raw.astype(F32)
    bias_full = jnp.repeat(jnp.swapaxes(b_bias_s, 1, 2), DG_B, axis=2)

    def rows(p):
        return p[:, None, :]

    mix_g, mix_b, ffn_g, ffn_b = rows(ln_mix_g), rows(ln_mix_b), rows(ln_ffn_g), rows(ln_ffn_b)
    norm_g, v_g, v_b = rows(a_norm_g), rows(b_ln_g), rows(b_ln_b)

    xp = x_prompt
    xs = x_sample.reshape(n_dec, d)
    st_p, v_p, v_s = [], [], []
    st_s = None
    for layer in range(DEPTH):
        j = layer // N_MIXERS
        if layer % N_MIXERS == 0:
            xp, s = _hgrn_prompt(xp, layer, lb_raw, a_w_in_b, norm_g, a_w_out_b, mix_g, mix_b,
                                 masks, signs)
            st_p.append(s)
            xs, st_s = _hgrn_decode(xs, layer, lb_raw, a_w_in_b, norm_g, a_w_out_b, mix_g, mix_b,
                                    state_hgrn, st_s)
        else:
            xp, v, xs, vd = _cmlp(xp, xs, j, layer, b_w_in_b, v_g, v_b, b_w_s, bias_full, b_w_out_b,
                                  mix_g, mix_b)
            v_p.append(v)
            v_s.append(vd[:, None, :])
        xp, xs = _ffn(xp.reshape(bsz * seq, d), xs, layer, ffn_w_in_b, ffn_w_out_b, ffn_g, ffn_b)
        xp = xp.reshape(bsz, seq, d)

    return (xp, xs.reshape(n_dec, 1, d), jnp.stack(st_p), st_s, jnp.stack(v_p), jnp.stack(v_s))
```

```python
import functools

import numpy as np
import jax
import jax.numpy as jnp
from jax import lax
from jax.experimental import pallas as pl
from jax.experimental.pallas import tpu as pltpu

D_MODEL = 1024
DEPTH = 4
N_MIXERS = 2
DK_A = 128
H_A = D_MODEL // DK_A
DV_A = D_MODEL // H_A
D_INNER_B = D_MODEL
CHUNK_B = 128
G_B = 8
DG_B = D_INNER_B // G_B
D_FF = -(-(8 * D_MODEL) // (3 * 256)) * 256
ALPHA = (2 * DEPTH) ** 0.25
LN_EPS = 1e-5
RMS_EPS = 1e-6

LANES = 128
SUBLANES = 8
MXU_DIM = 256
VMEM_LIMIT_BYTES = 56 * 2**20

ROW_TILE = 512
FFN_ROW_TILE = 512
CHUNK_A = 128
HEAD_GROUP = 8
SUB_A = 32
DECAY_LIMIT_LOG2 = 96.0
FF_TILE = MXU_DIM
DEC_SEQS = 8
DEC_SEQ_GROUP = 4

BF16 = jnp.bfloat16
F32 = jnp.float32


def _level_halves(chunk):
    out, h = [], chunk // 2
    while h >= 1:
        out.append(h)
        h //= 2
    return tuple(out)


LEVEL_HALVES = _level_halves(CHUNK_A)


def _level_masks(chunk):
    t = np.arange(chunk)[:, None]
    s = np.arange(chunk)[None, :]
    masks = [((t // (2 * h) == s // (2 * h)) & (t % (2 * h) >= h) & (s % (2 * h) < h))
             for h in _level_halves(chunk)]
    masks.append(t == s)
    masks.append((t // SUB_A == s // SUB_A) & (t >= s))
    return np.stack(masks).astype(np.float32)


def _level_signs(chunk, width):
    t = np.arange(chunk)[:, None]
    signs = [np.where(t % (2 * h) >= h, 1.0, -1.0) for h in _level_halves(chunk)]
    return np.broadcast_to(np.stack(signs), (len(signs), chunk, width)).astype(np.float32)


def _dot(a, b):
    return jnp.dot(a, b, preferred_element_type=F32)


def _dot_nt(a, b):
    return lax.dot_general(a, b, (((1,), (1,)), ((), ())), preferred_element_type=F32)


def _dot_tn(a, b):
    return lax.dot_general(a, b, (((0,), (0,)), ((), ())), preferred_element_type=F32)


def _layer_norm(z, g, b):
    mu = jnp.mean(z, axis=-1, keepdims=True)
    zc = z - mu
    var = jnp.mean(zc * zc, axis=-1, keepdims=True)
    return zc * lax.rsqrt(var + LN_EPS) * g + b


def _gelu(x):
    return 0.5 * x * (1.0 + lax.erf(x * (2.0 ** -0.5)))


def _lower_bound(raw, layer):
    m = jnp.max(raw, axis=0, keepdims=True)
    e = jnp.exp(raw - m)
    p = e / jnp.sum(e, axis=0, keepdims=True)
    c = p[0:1]
    for i in range(1, layer + 1):
        c = c + p[i:i + 1]
    return c - p[0:1]


def _hgrn_gates(xb, win_ref, lb):
    d = D_MODEL
    q = jax.nn.silu(_dot(xb, win_ref[:, 0:d])) * (DK_A ** -0.5)
    f = lb + (1.0 - lb) * jax.nn.sigmoid(_dot(xb, win_ref[:, d:2 * d]))
    v = _dot(xb, win_ref[:, 2 * d:3 * d])
    gate = jax.nn.silu(_dot(xb, win_ref[:, 3 * d:4 * d]))
    return q, f, v, gate


def _split2(a):
    hi = a.astype(BF16)
    lo = (a - hi.astype(F32)).astype(BF16)
    return jnp.concatenate([hi, lo], axis=1)


def _level_ref(g, half):
    n, w = g.shape
    if half >= SUBLANES // 2:
        blk = 2 * half
        parts = [jnp.broadcast_to(g[b * blk + half - 1:b * blk + half, :], (blk, w))
                 for b in range(n // blk)]
        return parts[0] if len(parts) == 1 else jnp.concatenate(parts, axis=0)
    row = lax.broadcasted_iota(jnp.int32, g.shape, 0)
    if half == 2:
        m = row % 4
        return jnp.where(m == 0, pltpu.roll(g, n - 1, axis=0),
                         jnp.where(m == 1, g,
                                   jnp.where(m == 2, pltpu.roll(g, 1, axis=0),
                                             pltpu.roll(g, 2, axis=0))))
    assert half == 1
    return jnp.where(row % 2 == 0, g, pltpu.roll(g, 1, axis=0))


def _sub_block_ref(g):
    n, w = g.shape
    parts = [jnp.zeros((SUB_A, w), g.dtype)]
    parts += [jnp.broadcast_to(g[b * SUB_A - 1:b * SUB_A, :], (SUB_A, w))
              for b in range(1, n // SUB_A)]
    return jnp.concatenate(parts, axis=0)


def _hgrn_group_chunk(stable, heads, c, tri, q_s, k_s, v_s, lf_s, oh_s, st_ref, mask_ref, sign_ref,
                      norm_g):
    if isinstance(c, int):
        rows = slice(c * CHUNK_A, (c + 1) * CHUNK_A)
    else:
        rows = pl.ds(pl.multiple_of(c * CHUNK_A, CHUNK_A), CHUNK_A)
    n = range(len(heads))
    q = [q_s[h, rows, :] for h in heads]
    k = [k_s[h, rows, :] for h in heads]
    vb = [v_s[h, rows, :] for h in heads]
    st = [st_ref[h] for h in heads]
    g2 = [_dot(tri, lf_s[h, rows, :]) for h in heads]
    g = [a[:, :DK_A] + a[:, DK_A:] for a in g2]
    g_last = [a[CHUNK_A - 1:CHUNK_A, :] for a in g]
    o = [_dot_nt(q[i] * jnp.exp2(g[i]).astype(BF16), st[i].astype(BF16)) for i in n]
    ks = [k[i] * jnp.exp2(g_last[i] - g[i]).astype(BF16) for i in n]
    st_new = [st[i] * jnp.exp2(g_last[i]) + _dot_tn(vb[i], ks[i]) for i in n]
    if stable:
        levels = [(lvl, half) for lvl, half in enumerate(LEVEL_HALVES) if 2 * half > SUB_A]
        near = mask_ref[len(LEVEL_HALVES) + 1]
        d = [g[i] - _sub_block_ref(g[i]) for i in n]
        a = [near * _dot_nt(q[i] * jnp.exp2(d[i]).astype(BF16),
                            k[i] * jnp.exp2(-d[i]).astype(BF16)) for i in n]
    else:
        levels = list(enumerate(LEVEL_HALVES))
        diag = mask_ref[len(LEVEL_HALVES)]
        a = [diag * _dot_nt(q[i], k[i]) for i in n]
    for lvl, half in levels:
        e = [jnp.exp2((g[i] - _level_ref(g[i], half)) * sign_ref[lvl]).astype(BF16) for i in n]
        p = [_dot_nt(q[i] * e[i], k[i] * e[i]) for i in n]
        a = [a[i] + mask_ref[lvl] * p[i] for i in n]
    o = [o[i] + _dot(a[i].astype(BF16), vb[i]) for i in n]
    ms = [jnp.mean(o[i] * o[i], axis=-1, keepdims=True) for i in n]
    for i, h in enumerate(heads):
        st_ref[h] = st_new[i]
        oh_s[h, rows, :] = o[i] * lax.rsqrt(ms[i] + RMS_EPS) * norm_g


def _hgrn_prompt_kernel(layer, x_ref, lbraw_ref, win_ref, ng_ref, wout_ref, lng_ref, lnb_ref,
                        mask_ref, sign_ref, o_ref, sfin_ref, st_ref, q_s, k_s, v_s, lf_s, oh_s,
                        gate_s):
    t = pl.program_id(1)

    @pl.when(t == 0)
    def _():
        st_ref[...] = jnp.zeros_like(st_ref)

    x = x_ref[0]
    q, f, v, gate = _hgrn_gates(x.astype(BF16), win_ref, _lower_bound(lbraw_ref[...], layer))
    k = 1.0 - f
    lf = jnp.log2(f)
    gate_s[...] = gate
    for h in range(H_A):
        ls = slice(h * DK_A, (h + 1) * DK_A)
        q_s[h] = q[:, ls].astype(BF16)
        k_s[h] = k[:, ls].astype(BF16)
        v_s[h] = v[:, ls].astype(BF16)
        lf_s[h] = _split2(lf[:, ls])

    norm_g = ng_ref[...]
    row = lax.broadcasted_iota(jnp.int32, (CHUNK_A, CHUNK_A), 0)
    col = lax.broadcasted_iota(jnp.int32, (CHUNK_A, CHUNK_A), 1)
    tri = (row >= col).astype(BF16)

    sub_decay = jnp.sum(lf.reshape(ROW_TILE // SUB_A, SUB_A, D_MODEL), axis=1)
    stable = jnp.min(sub_decay) >= -DECAY_LIMIT_LOG2

    def finish_rows(rows):
        o = jnp.concatenate([oh_s[h, rows, :] for h in range(H_A)], axis=1) * gate_s[rows, :]
        y = _dot(o.astype(BF16), wout_ref[...])
        o_ref[0, rows, :] = _layer_norm(ALPHA * x_ref[0, rows, :] + y, lng_ref[...], lnb_ref[...])

    def run_stable():
        heads = list(range(H_A))
        for c in range(ROW_TILE // CHUNK_A):
            _hgrn_group_chunk(True, heads, c, tri, q_s, k_s, v_s, lf_s, oh_s, st_ref, mask_ref,
                              sign_ref, norm_g)
            if c > 0:
                finish_rows(slice((c - 1) * CHUNK_A, c * CHUNK_A))
        finish_rows(slice(ROW_TILE - CHUNK_A, ROW_TILE))

    def run_general():
        def group_body(hg, carry):
            heads = [hg * HEAD_GROUP + i for i in range(HEAD_GROUP)]

            def chunk_body(c, carry2):
                _hgrn_group_chunk(False, heads, c, tri, q_s, k_s, v_s, lf_s, oh_s, st_ref,
                                  mask_ref, sign_ref, norm_g)
                return carry2
            return lax.fori_loop(0, ROW_TILE // CHUNK_A, chunk_body, carry)

        lax.fori_loop(0, H_A // HEAD_GROUP, group_body, 0)
        finish_rows(slice(0, ROW_TILE))

    lax.cond(stable, run_stable, run_general)

    @pl.when(t == pl.num_programs(1) - 1)
    def _():
        for h in range(H_A):
            sfin_ref[0, h] = st_ref[h].T


def _const_spec(shape):
    nd = len(shape)
    return pl.BlockSpec(shape, lambda *_: (0,) * nd, pipeline_mode=pl.Buffered(1))


def _layer_spec(shape, j):
    nd = len(shape)
    return pl.BlockSpec((None,) + tuple(shape[1:]), lambda *_: (j,) + (0,) * (nd - 1),
                        pipeline_mode=pl.Buffered(1))


def _params(n_axes):
    return pltpu.CompilerParams(dimension_semantics=("arbitrary",) * n_axes,
                                vmem_limit_bytes=VMEM_LIMIT_BYTES)


def _hgrn_prompt(x, layer, lb_raw, w_in, norm_g, w_out, ln_g, ln_b, masks, signs):
    b, l, d = x.shape
    j = layer // N_MIXERS
    assert l % ROW_TILE == 0 and ROW_TILE % CHUNK_A == 0 and H_A % HEAD_GROUP == 0
    head_operand = pltpu.VMEM((H_A, ROW_TILE, DK_A), BF16)
    return pl.pallas_call(
        functools.partial(_hgrn_prompt_kernel, layer),
        grid=(b, l // ROW_TILE),
        in_specs=[
            pl.BlockSpec((1, ROW_TILE, d), lambda i, t: (i, t, 0)),
            _const_spec(lb_raw.shape),
            _layer_spec(w_in.shape, j),
            _layer_spec(norm_g.shape, j),
            _layer_spec(w_out.shape, j),
            _layer_spec(ln_g.shape, layer),
            _layer_spec(ln_b.shape, layer),
            _const_spec(masks.shape),
            _const_spec(signs.shape),
        ],
        out_specs=[
            pl.BlockSpec((1, ROW_TILE, d), lambda i, t: (i, t, 0)),
            pl.BlockSpec((1, H_A, DK_A, DV_A), lambda i, t: (i, 0, 0, 0)),
        ],
        out_shape=[
            jax.ShapeDtypeStruct((b, l, d), F32),
            jax.ShapeDtypeStruct((b, H_A, DK_A, DV_A), F32),
        ],
        scratch_shapes=[
            pltpu.VMEM((H_A, DV_A, DK_A), F32),
            head_operand, head_operand, head_operand,
            pltpu.VMEM((H_A, ROW_TILE, 2 * DK_A), BF16),
            pltpu.VMEM((H_A, ROW_TILE, DV_A), F32),
            pltpu.VMEM((ROW_TILE, d), F32),
        ],
        compiler_params=_params(2),
        name="hgrn_prompt",
    )(x, lb_raw, w_in, norm_g, w_out, ln_g, ln_b, masks, signs)


def _split3(f):
    hi = f.astype(BF16).astype(F32)
    r = f - hi
    mid = r.astype(BF16).astype(F32)
    lo = (r - mid).astype(BF16).astype(F32)
    return hi, mid, lo


def _hgrn_decode_kernel(layer, chained, x_ref, lbraw_ref, win_ref, ng_ref, wout_ref, lng_ref,
                        lnb_ref, s_ref, *rest):
    o_ref, snew_ref, f3_s, k_s, q_s, v_s, gate_s, oacc_s = rest[1:] if chained else rest

    @pl.when(pl.program_id(0) == 0)
    def _():
        _hgrn_decode_step(layer, x_ref, lbraw_ref, win_ref, ng_ref, wout_ref, lng_ref, lnb_ref,
                          s_ref, o_ref, snew_ref, f3_s, k_s, q_s, v_s, gate_s, oacc_s)

    @pl.when(pl.program_id(0) > 0)
    def _():
        snew_ref[...] = jnp.zeros_like(snew_ref)


def _hgrn_decode_step(layer, x_ref, lbraw_ref, win_ref, ng_ref, wout_ref, lng_ref, lnb_ref,
                      s_ref, o_ref, snew_ref, f3_s, k_s, q_s, v_s, gate_s, oacc_s):
    step = pl.program_id(1)
    d = D_MODEL
    nb = DEC_SEQS

    @pl.when(step == 0)
    def _():
        q, f, v, gate = _hgrn_gates(x_ref[...].astype(BF16), win_ref,
                                    _lower_bound(lbraw_ref[...], layer))
        hi, mid, lo = _split3(f)
        f3_s[0] = hi
        f3_s[1] = mid
        f3_s[2] = lo
        k_s[...] = 1.0 - f
        q_s[...] = q
        v_s[...] = v
        gate_s[...] = gate

    rows = pl.ds(pl.multiple_of(step * nb, nb), nb)
    fhi = f3_s[0, rows, :]
    fmid = f3_s[1, rows, :]
    flo = f3_s[2, rows, :]
    k8 = k_s[rows, :]
    q8 = q_s[rows, :]
    v8 = v_s[rows, :]
    ones = jnp.ones((nb, DV_A), F32)
    zeros = jnp.zeros((nb, DV_A), F32)
    rhs = []
    for h in range(H_A):
        ls = slice(h * DK_A, (h + 1) * DK_A)
        left = jnp.concatenate([ones, ones, ones, zeros], axis=0)
        right = jnp.concatenate([zeros, zeros, zeros, v8[:, ls]], axis=0)
        rhs.append(jnp.concatenate([left, right], axis=1).astype(BF16))
    seq = lax.broadcasted_iota(jnp.int32, (nb, DK_A), 0)
    head_lanes = [slice(h * DK_A, (h + 1) * DK_A) for h in range(H_A)]

    def group_body(p, o8):
        seqs = [p * DEC_SEQ_GROUP + i for i in range(DEC_SEQ_GROUP)]
        sels = [seq == i for i in seqs]
        chains = [(si, h) for si in range(DEC_SEQ_GROUP) for h in range(H_A)]
        s_old = [s_ref[seqs[si], h] for si, h in chains]
        lhs = [jnp.concatenate([jnp.where(sels[si], a[:, head_lanes[h]], 0.0)
                                for a in (fhi, fmid, flo, k8)], axis=0).astype(BF16)
               for si, h in chains]
        fb_kv = [_dot_tn(lhs[c], rhs[h]) for c, (si, h) in enumerate(chains)]
        s_new = [fb_kv[c][:, :DV_A] * s_old[c] + fb_kv[c][:, DV_A:] for c in range(len(chains))]
        qsel = [jnp.where(sels[si], q8[:, head_lanes[h]], 0.0).astype(BF16) for si, h in chains]
        outs = [_dot(qsel[c], s_new[c].astype(BF16)) for c in range(len(chains))]
        for c, (si, h) in enumerate(chains):
            snew_ref[seqs[si], h] = s_new[c]
        for si in range(DEC_SEQ_GROUP):
            o8 = o8 + jnp.concatenate(outs[si * H_A:(si + 1) * H_A], axis=1)
        return o8

    oacc_s[rows, :] = lax.fori_loop(0, nb // DEC_SEQ_GROUP, group_body, jnp.zeros((nb, d), F32))

    @pl.when(step == pl.num_programs(1) - 1)
    def _():
        o = oacc_s[...]
        norm_g = ng_ref[...]
        heads = []
        for h in range(H_A):
            oh = o[:, h * DV_A:(h + 1) * DV_A]
            ms = jnp.mean(oh * oh, axis=-1, keepdims=True)
            heads.append(oh * lax.rsqrt(ms + RMS_EPS) * norm_g)
        on = jnp.concatenate(heads, axis=1) * gate_s[...]
        y = _dot(on.astype(BF16), wout_ref[...])
        o_ref[...] = _layer_norm(ALPHA * x_ref[...] + y, lng_ref[...], lnb_ref[...])


def _hgrn_decode(x, layer, lb_raw, w_in, norm_g, w_out, ln_g, ln_b, states, new_states):
    n, d = x.shape
    j = layer // N_MIXERS
    assert n % DEC_SEQS == 0 and DEC_SEQS % DEC_SEQ_GROUP == 0
    chained = new_states is not None
    steps = n // DEC_SEQS
    passes = 1 if chained else states.shape[0] - j
    row_scratch = pltpu.VMEM((n, d), F32)
    state_block = (None, DEC_SEQS, H_A, DK_A, DV_A)
    state_in = pl.BlockSpec(state_block,
                            lambda g, s: (j, jnp.where(g == 0, s, steps - 1), 0, 0, 0))
    state_out = pl.BlockSpec(state_block, lambda g, s: (j + g, s, 0, 0, 0))
    operands = [x, lb_raw, w_in, norm_g, w_out, ln_g, ln_b, states]
    in_specs = [
        _const_spec(x.shape),
        _const_spec(lb_raw.shape),
        _layer_spec(w_in.shape, j),
        _layer_spec(norm_g.shape, j),
        _layer_spec(w_out.shape, j),
        _layer_spec(ln_g.shape, layer),
        _layer_spec(ln_b.shape, layer),
        state_in,
    ]
    if chained:
        operands.append(new_states)
        in_specs.append(pl.BlockSpec(memory_space=pl.ANY))
    return pl.pallas_call(
        functools.partial(_hgrn_decode_kernel, layer, chained),
        grid=(passes, steps),
        in_specs=in_specs,
        out_specs=[pl.BlockSpec((n, d), lambda g, s: (0, 0)), state_out],
        out_shape=[
            jax.ShapeDtypeStruct((n, d), F32),
            jax.ShapeDtypeStruct(states.shape, F32),
        ],
        scratch_shapes=[
            pltpu.VMEM((3, n, d), F32),
            row_scratch, row_scratch, row_scratch, row_scratch, row_scratch,
        ],
        input_output_aliases={len(operands) - 1: 1} if chained else {},
        compiler_params=_params(2),
        name="hgrn_decode",
    )(*operands)


def _cmlp_gate_values(x, win_ref, vg_ref, vb_ref):
    xb = x.astype(BF16)
    di = D_INNER_B
    u = _gelu(_dot(xb, win_ref[:, 0:di]))
    v = _gelu(_dot(xb, win_ref[:, di:2 * di]))
    return u, _layer_norm(v, vg_ref[...], vb_ref[...])


def _cmlp_finish(x, u, mixed, wout_ref, lng_ref, lnb_ref):
    y = _dot((u * mixed).astype(BF16), wout_ref[...])
    return _layer_norm(ALPHA * x + y, lng_ref[...], lnb_ref[...])


def _cmlp_kernel(x_ref, xdec_ref, win_ref, vg_ref, vb_ref, ws_ref, bias_ref, wout_ref,
                 lng_ref, lnb_ref, o_ref, vrows_ref, odec_ref, vdec_ref):
    x = x_ref[...]
    n = x.shape[0]
    u, v = _cmlp_gate_values(x, win_ref, vg_ref, vb_ref)
    row = lax.broadcasted_iota(jnp.int32, (CHUNK_B, CHUNK_B), 0)
    col = lax.broadcasted_iota(jnp.int32, (CHUNK_B, CHUNK_B), 1)
    w = [jnp.where(row >= col, ws_ref[g], 0.0).astype(BF16) for g in range(G_B)]
    bias = bias_ref[...]
    vb16 = v.astype(BF16)
    chunks = []
    for c in range(n // CHUNK_B):
        rs = slice(c * CHUNK_B, (c + 1) * CHUNK_B)
        groups = [_dot(w[g], vb16[rs, g * DG_B:(g + 1) * DG_B]) for g in range(G_B)]
        chunks.append(jnp.concatenate(groups, axis=1) + bias)
    o_ref[...] = _cmlp_finish(x, u, jnp.concatenate(chunks, axis=0), wout_ref, lng_ref, lnb_ref)

    @pl.when(pl.program_id(1) == pl.num_programs(1) - 1)
    def _():
        vrows_ref[...] = v[n - CHUNK_B:, :]

    @pl.when((pl.program_id(0) == 0) & (pl.program_id(1) == 0))
    def _():
        xd = xdec_ref[...]
        ud, vd = _cmlp_gate_values(xd, win_ref, vg_ref, vb_ref)
        w00 = jnp.concatenate([jnp.broadcast_to(ws_ref[g, 0:1, 0:1], (1, DG_B))
                               for g in range(G_B)], axis=1)
        vdec_ref[...] = vd
        odec_ref[...] = _cmlp_finish(xd, ud, vd * w00 + bias[0:1, :], wout_ref, lng_ref, lnb_ref)


def _cmlp(x, x_dec, j, layer, w_in, v_g, v_b, w_s, bias_full, w_out, ln_g, ln_b):
    b, l, d = x.shape
    assert l % ROW_TILE == 0 and ROW_TILE % CHUNK_B == 0
    dec_out = pl.BlockSpec(x_dec.shape, lambda i, t: (0, 0))
    return pl.pallas_call(
        _cmlp_kernel,
        grid=(b, l // ROW_TILE),
        in_specs=[
            pl.BlockSpec((None, ROW_TILE, d), lambda i, t: (i, t, 0)),
            _const_spec(x_dec.shape),
            _layer_spec(w_in.shape, j),
            _layer_spec(v_g.shape, j),
            _layer_spec(v_b.shape, j),
            _layer_spec(w_s.shape, j),
            _layer_spec(bias_full.shape, j),
            _layer_spec(w_out.shape, j),
            _layer_spec(ln_g.shape, layer),
            _layer_spec(ln_b.shape, layer),
        ],
        out_specs=[
            pl.BlockSpec((None, ROW_TILE, d), lambda i, t: (i, t, 0)),
            pl.BlockSpec((None, CHUNK_B, D_INNER_B), lambda i, t: (i, 0, 0)),
            dec_out,
            dec_out,
        ],
        out_shape=[
            jax.ShapeDtypeStruct((b, l, d), F32),
            jax.ShapeDtypeStruct((b, CHUNK_B, D_INNER_B), F32),
            jax.ShapeDtypeStruct(x_dec.shape, F32),
            jax.ShapeDtypeStruct(x_dec.shape, F32),
        ],
        compiler_params=_params(2),
        name="cmlp",
    )(x, x_dec, w_in, v_g, v_b, w_s, bias_full, w_out, ln_g, ln_b)


def _ffn_rows(x, win_ref, wout_ref, ln_g, ln_b):
    acc = jnp.zeros(x.shape, F32)
    for j in range(D_FF // FF_TILE):
        cs = slice(j * FF_TILE, (j + 1) * FF_TILE)
        us = slice(D_FF + j * FF_TILE, D_FF + (j + 1) * FF_TILE)
        hcol = jax.nn.silu(_dot(x, win_ref[:, cs])) * _dot(x, win_ref[:, us])
        acc = acc + _dot(hcol, wout_ref[cs, :])
    return _layer_norm(ALPHA * x + acc, ln_g, ln_b)


def _ffn_kernel(x_ref, xdec_ref, win_ref, wout_ref, lng_ref, lnb_ref, o_ref, odec_ref):
    ln_g, ln_b = lng_ref[...], lnb_ref[...]
    o_ref[...] = _ffn_rows(x_ref[...], win_ref, wout_ref, ln_g, ln_b)

    @pl.when(pl.program_id(0) == 0)
    def _():
        odec_ref[...] = _ffn_rows(xdec_ref[...], win_ref, wout_ref, ln_g, ln_b)


def _ffn(x, x_dec, layer, w_in, w_out, ln_g, ln_b):
    n, d = x.shape
    assert n % FFN_ROW_TILE == 0 and D_FF % FF_TILE == 0
    return pl.pallas_call(
        _ffn_kernel,
        grid=(n // FFN_ROW_TILE,),
        in_specs=[
            pl.BlockSpec((FFN_ROW_TILE, d), lambda i: (i, 0)),
            _const_spec(x_dec.shape),
            _layer_spec(w_in.shape, layer),
            _layer_spec(w_out.shape, layer),
            _layer_spec(ln_g.shape, layer),
            _layer_spec(ln_b.shape, layer),
        ],
        out_specs=[
            pl.BlockSpec((FFN_ROW_TILE, d), lambda i: (i, 0)),
            pl.BlockSpec(x_dec.shape, lambda i: (0, 0)),
        ],
        out_shape=[
            jax.ShapeDtypeStruct((n, d), F32),
            jax.ShapeDtypeStruct(x_dec.shape, F32),
        ],
        compiler_params=_params(1),
        name="ffn",
    )(x, x_dec, w_in, w_out, ln_g, ln_b)


def kernel(x_prompt, x_sample, state_hgrn, ln_mix_g, ln_mix_b, ln_ffn_g, ln_ffn_b, a_lb_raw, a_w_in, a_norm_g, a_w_out, b_w_in, b_ln_g, b_ln_b, b_w_s, b_bias_s, b_w_out, ffn_w_in, ffn_w_out):
    bsz, seq, d = x_prompt.shape
    n_dec = x_sample.shape[0]
    assert x_sample.shape[1] == 1 and seq % CHUNK_B == 0

    a_w_in_b = a_w_in.astype(BF16)
    a_w_out_b = a_w_out.astype(BF16)
    b_w_in_b = b_w_in.astype(BF16)
    b_w_out_b = b_w_out.astype(BF16)
    masks = jnp.asarray(_level_masks(CHUNK_A))
    signs = jnp.asarray(_level_signs(CHUNK_A, DK_A))
    lb_raw = a_lb_raw.astype(F32)
    bias_full = jnp.repeat(jnp.swapaxes(b_bias_s, 1, 2), DG_B, axis=2)

    def rows(p):
        return p[:, None, :]

    mix_g, mix_b, ffn_g, ffn_b = rows(ln_mix_g), rows(ln_mix_b), rows(ln_ffn_g), rows(ln_ffn_b)
    norm_g, v_g, v_b = rows(a_norm_g), rows(b_ln_g), rows(b_ln_b)

    xp = x_prompt
    xs = x_sample.reshape(n_dec, d)
    st_p, v_p, v_s = [], [], []
    st_s = None
    for layer in range(DEPTH):
        j = layer // N_MIXERS
        if layer % N_MIXERS == 0:
            xp, s = _hgrn_prompt(xp, layer, lb_raw, a_w_in_b, norm_g, a_w_out_b, mix_g, mix_b,
                                 masks, signs)
            st_p.append(s)
            xs, st_s = _hgrn_decode(xs, layer, lb_raw, a_w_in_b, norm_g, a_w_out_b, mix_g, mix_b,
                                    state_hgrn, st_s)
        else:
            xp, v, xs, vd = _cmlp(xp, xs, j, layer, b_w_in_b, v_g, v_b, b_w_s, bias_full, b_w_out_b,
                                  mix_g, mix_b)
            v_p.append(v)
            v_s.append(vd[:, None, :])
        xp, xs = _ffn(xp.reshape(bsz * seq, d), xs, layer, ffn_w_in, ffn_w_out, ffn_g, ffn_b)
        xp = xp.reshape(bsz, seq, d)

    return (xp, xs.reshape(n_dec, 1, d), jnp.stack(st_p), st_s, jnp.stack(v_p), jnp.stack(v_s))
```

```python
import functools

import numpy as np
import jax
import jax.numpy as jnp
from jax import lax
from jax.experimental import pallas as pl
from jax.experimental.pallas import tpu as pltpu

D_MODEL = 1024
DEPTH = 4
N_MIXERS = 2
DK_A = 128
H_A = D_MODEL // DK_A
DV_A = D_MODEL // H_A
D_INNER_B = D_MODEL
CHUNK_B = 128
G_B = 8
DG_B = D_INNER_B // G_B
D_FF = -(-(8 * D_MODEL) // (3 * 256)) * 256
ALPHA = (2 * DEPTH) ** 0.25
LN_EPS = 1e-5
RMS_EPS = 1e-6

LANES = 128
SUBLANES = 8
MXU_DIM = 256
VMEM_LIMIT_BYTES = 56 * 2**20

ROW_TILE = 512
FFN_ROW_TILE = 512
CHUNK_A = 128
HEAD_GROUP = 8
STABLE_CHUNK_A = 64
DECAY_LIMIT_LOG2 = 118.0
STABLE_CHUNK_GROUP = 4
FINISH_ROWS_A = 256
FF_TILE = MXU_DIM
DEC_SEQS = 8
DEC_SEQ_GROUP = 4

BF16 = jnp.bfloat16
F32 = jnp.float32


def _level_halves(chunk):
    out, h = [], chunk // 2
    while h >= 1:
        out.append(h)
        h //= 2
    return tuple(out)


LEVEL_HALVES = _level_halves(CHUNK_A)


def _level_masks(chunk):
    t = np.arange(chunk)[:, None]
    s = np.arange(chunk)[None, :]
    masks = [((t // (2 * h) == s // (2 * h)) & (t % (2 * h) >= h) & (s % (2 * h) < h))
             for h in _level_halves(chunk)]
    masks.append(t == s)
    return np.stack(masks).astype(np.float32)


def _level_signs(chunk, width):
    t = np.arange(chunk)[:, None]
    signs = [np.where(t % (2 * h) >= h, 1.0, -1.0) for h in _level_halves(chunk)]
    return np.broadcast_to(np.stack(signs), (len(signs), chunk, width)).astype(np.float32)


def _dot(a, b):
    return lax.dot_general(a, b, (((1,), (0,)), ((), ())), preferred_element_type=F32)


def _dot_nt(a, b):
    return lax.dot_general(a, b, (((1,), (1,)), ((), ())), preferred_element_type=F32)


def _dot_tn(a, b):
    return lax.dot_general(a, b, (((0,), (0,)), ((), ())), preferred_element_type=F32)


def _layer_norm(z, g, b):
    mu = jnp.mean(z, axis=-1, keepdims=True)
    zc = z - mu
    var = jnp.mean(zc * zc, axis=-1, keepdims=True)
    return zc * lax.rsqrt(var + LN_EPS) * g + b


def _gelu(x):
    return 0.5 * x * (1.0 + lax.erf(x * (2.0 ** -0.5)))


def _lower_bound(raw, layer):
    m = jnp.max(raw, axis=0, keepdims=True)
    e = jnp.exp(raw - m)
    p = e / jnp.sum(e, axis=0, keepdims=True)
    c = p[0:1]
    for i in range(1, layer + 1):
        c = c + p[i:i + 1]
    return c - p[0:1]


def _hgrn_gates(xb, win_ref, lb):
    d = D_MODEL
    q = jax.nn.silu(_dot(xb, win_ref[:, 0:d])) * (DK_A ** -0.5)
    f = lb + (1.0 - lb) * jax.nn.sigmoid(_dot(xb, win_ref[:, d:2 * d]))
    v = _dot(xb, win_ref[:, 2 * d:3 * d])
    gate = jax.nn.silu(_dot(xb, win_ref[:, 3 * d:4 * d]))
    return q, f, v, gate


def _split2(a):
    hi = a.astype(BF16)
    lo = (a - hi.astype(F32)).astype(BF16)
    return jnp.concatenate([hi, lo], axis=1)


def _level_ref(g, half):
    n, w = g.shape
    if half >= SUBLANES // 2:
        blk = 2 * half
        parts = [jnp.broadcast_to(g[b * blk + half - 1:b * blk + half, :], (blk, w))
                 for b in range(n // blk)]
        return parts[0] if len(parts) == 1 else jnp.concatenate(parts, axis=0)
    row = lax.broadcasted_iota(jnp.int32, g.shape, 0)
    if half == 2:
        m = row % 4
        return jnp.where(m == 0, pltpu.roll(g, n - 1, axis=0),
                         jnp.where(m == 1, g,
                                   jnp.where(m == 2, pltpu.roll(g, 1, axis=0),
                                             pltpu.roll(g, 2, axis=0))))
    assert half == 1
    return jnp.where(row % 2 == 0, g, pltpu.roll(g, 1, axis=0))


def _rms_norm_rows(o, norm_g):
    ms = jnp.mean(o * o, axis=-1, keepdims=True)
    return o * lax.rsqrt(ms + RMS_EPS) * norm_g


def _hgrn_stable_chunks(heads, chunks, tri, causal, q_s, k_s, v_s, lf_s, oh_s, st_ref, norm_g):
    n_rows = STABLE_CHUNK_A
    items = [(c, h) for c in chunks for h in heads]
    n = range(len(items))
    rows = [slice(c * n_rows, (c + 1) * n_rows) for c, _ in items]
    q = [q_s[h, rows[i], :] for i, (_, h) in enumerate(items)]
    k = [k_s[h, rows[i], :] for i, (_, h) in enumerate(items)]
    vb = [v_s[h, rows[i], :] for i, (_, h) in enumerate(items)]
    g2 = [_dot(tri, lf_s[h, rows[i], :]) for i, (_, h) in enumerate(items)]
    g = [a[:, :DK_A] + a[:, DK_A:] for a in g2]
    g_last = [a[n_rows - 1:n_rows, :] for a in g]
    qp = [q[i] * jnp.exp2(g[i]).astype(BF16) for i in n]
    ks = [k[i] * jnp.exp2(g_last[i] - g[i]).astype(BF16) for i in n]
    kv = [_dot_tn(vb[i], ks[i]) for i in n]
    a = [causal * _dot_nt(qp[i], k[i] * jnp.exp2(-g[i]).astype(BF16)) for i in n]
    intra = [_dot(a[i].astype(BF16), vb[i]) for i in n]
    st = {h: st_ref[h] for h in heads}
    for i, (_, h) in enumerate(items):
        o = intra[i] + _dot_nt(qp[i], st[h].astype(BF16))
        oh_s[h, rows[i], :] = _rms_norm_rows(o, norm_g)
        st[h] = st[h] * jnp.exp2(g_last[i]) + kv[i]
    for h in heads:
        st_ref[h] = st[h]


def _hgrn_general_chunk(heads, c, tri, q_s, k_s, v_s, lf_s, oh_s, st_ref, mask_ref, sign_ref,
                        norm_g):
    rows = pl.ds(pl.multiple_of(c * CHUNK_A, CHUNK_A), CHUNK_A)
    n = range(len(heads))
    q = [q_s[h, rows, :] for h in heads]
    k = [k_s[h, rows, :] for h in heads]
    vb = [v_s[h, rows, :] for h in heads]
    st = [st_ref[h] for h in heads]
    g2 = [_dot(tri, lf_s[h, rows, :]) for h in heads]
    g = [a[:, :DK_A] + a[:, DK_A:] for a in g2]
    g_last = [a[CHUNK_A - 1:CHUNK_A, :] for a in g]
    o = [_dot_nt(q[i] * jnp.exp2(g[i]).astype(BF16), st[i].astype(BF16)) for i in n]
    ks = [k[i] * jnp.exp2(g_last[i] - g[i]).astype(BF16) for i in n]
    st_new = [st[i] * jnp.exp2(g_last[i]) + _dot_tn(vb[i], ks[i]) for i in n]
    diag = mask_ref[len(LEVEL_HALVES)]
    a = [diag * _dot_nt(q[i], k[i]) for i in n]
    for lvl, half in enumerate(LEVEL_HALVES):
        e = [jnp.exp2((g[i] - _level_ref(g[i], half)) * sign_ref[lvl]).astype(BF16) for i in n]
        p = [_dot_nt(q[i] * e[i], k[i] * e[i]) for i in n]
        a = [a[i] + mask_ref[lvl] * p[i] for i in n]
    o = [o[i] + _dot(a[i].astype(BF16), vb[i]) for i in n]
    for i, h in enumerate(heads):
        st_ref[h] = st_new[i]
        oh_s[h, rows, :] = _rms_norm_rows(o[i], norm_g)


def _hgrn_prompt_kernel(layer, x_ref, lbraw_ref, win_ref, ng_ref, wout_ref, lng_ref, lnb_ref,
                        mask_ref, sign_ref, o_ref, sfin_ref, st_ref, q_s, k_s, v_s, lf_s, oh_s,
                        gate_s):
    t = pl.program_id(1)

    @pl.when(t == 0)
    def _():
        st_ref[...] = jnp.zeros_like(st_ref)

    x = x_ref[0]
    q, f, v, gate = _hgrn_gates(x.astype(BF16), win_ref, _lower_bound(lbraw_ref[...], layer))
    k = 1.0 - f
    lf = jnp.log2(f)
    gate_s[...] = gate
    for h in range(H_A):
        ls = slice(h * DK_A, (h + 1) * DK_A)
        q_s[h] = q[:, ls].astype(BF16)
        k_s[h] = k[:, ls].astype(BF16)
        v_s[h] = v[:, ls].astype(BF16)
        lf_s[h] = _split2(lf[:, ls])

    norm_g = ng_ref[...]

    def lower_triangle(n_rows):
        row = lax.broadcasted_iota(jnp.int32, (n_rows, n_rows), 0)
        col = lax.broadcasted_iota(jnp.int32, (n_rows, n_rows), 1)
        return row >= col

    chunk_decay = jnp.sum(lf.reshape(ROW_TILE // STABLE_CHUNK_A, STABLE_CHUNK_A, D_MODEL), axis=1)
    stable = jnp.min(chunk_decay) >= -DECAY_LIMIT_LOG2

    def finish_rows(rows):
        o = jnp.concatenate([oh_s[h, rows, :] for h in range(H_A)], axis=1) * gate_s[rows, :]
        y = _dot(o.astype(BF16), wout_ref[...])
        o_ref[0, rows, :] = _layer_norm(ALPHA * x_ref[0, rows, :] + y, lng_ref[...], lnb_ref[...])

    def run_stable():
        causal = lower_triangle(STABLE_CHUNK_A)
        tri = causal.astype(BF16)
        causal = causal.astype(F32)
        heads = list(range(H_A))
        group = STABLE_CHUNK_GROUP
        for c0 in range(0, ROW_TILE // STABLE_CHUNK_A, group):
            _hgrn_stable_chunks(heads, list(range(c0, c0 + group)), tri, causal, q_s, k_s, v_s,
                                lf_s, oh_s, st_ref, norm_g)
            done = (c0 + group) * STABLE_CHUNK_A
            if done % FINISH_ROWS_A == 0:
                finish_rows(slice(done - FINISH_ROWS_A, done))

    def run_general():
        tri = lower_triangle(CHUNK_A).astype(BF16)

        def group_body(hg, carry):
            heads = [hg * HEAD_GROUP + i for i in range(HEAD_GROUP)]

            def chunk_body(c, carry2):
                _hgrn_general_chunk(heads, c, tri, q_s, k_s, v_s, lf_s, oh_s, st_ref, mask_ref,
                                    sign_ref, norm_g)
                return carry2
            return lax.fori_loop(0, ROW_TILE // CHUNK_A, chunk_body, carry)

        lax.fori_loop(0, H_A // HEAD_GROUP, group_body, 0)
        finish_rows(slice(0, ROW_TILE))

    lax.cond(stable, run_stable, run_general)

    @pl.when(t == pl.num_programs(1) - 1)
    def _():
        for h in range(H_A):
            sfin_ref[0, h] = st_ref[h].T


def _const_spec(shape):
    nd = len(shape)
    return pl.BlockSpec(shape, lambda *_: (0,) * nd, pipeline_mode=pl.Buffered(1))


def _layer_spec(shape, j):
    nd = len(shape)
    return pl.BlockSpec((None,) + tuple(shape[1:]), lambda *_: (j,) + (0,) * (nd - 1),
                        pipeline_mode=pl.Buffered(1))


def _params(n_axes):
    return pltpu.CompilerParams(dimension_semantics=("arbitrary",) * n_axes,
                                vmem_limit_bytes=VMEM_LIMIT_BYTES)


def _hgrn_prompt(x, layer, lb_raw, w_in, norm_g, w_out, ln_g, ln_b, masks, signs):
    b, l, d = x.shape
    j = layer // N_MIXERS
    assert l % ROW_TILE == 0 and ROW_TILE % CHUNK_A == 0 and H_A % HEAD_GROUP == 0
    assert ROW_TILE % FINISH_ROWS_A == 0
    assert FINISH_ROWS_A % (STABLE_CHUNK_GROUP * STABLE_CHUNK_A) == 0
    head_operand = pltpu.VMEM((H_A, ROW_TILE, DK_A), BF16)
    return pl.pallas_call(
        functools.partial(_hgrn_prompt_kernel, layer),
        grid=(b, l // ROW_TILE),
        in_specs=[
            pl.BlockSpec((1, ROW_TILE, d), lambda i, t: (i, t, 0)),
            _const_spec(lb_raw.shape),
            _layer_spec(w_in.shape, j),
            _layer_spec(norm_g.shape, j),
            _layer_spec(w_out.shape, j),
            _layer_spec(ln_g.shape, layer),
            _layer_spec(ln_b.shape, layer),
            _const_spec(masks.shape),
            _const_spec(signs.shape),
        ],
        out_specs=[
            pl.BlockSpec((1, ROW_TILE, d), lambda i, t: (i, t, 0)),
            pl.BlockSpec((1, H_A, DK_A, DV_A), lambda i, t: (i, 0, 0, 0)),
        ],
        out_shape=[
            jax.ShapeDtypeStruct((b, l, d), F32),
            jax.ShapeDtypeStruct((b, H_A, DK_A, DV_A), F32),
        ],
        scratch_shapes=[
            pltpu.VMEM((H_A, DV_A, DK_A), F32),
            head_operand, head_operand, head_operand,
            pltpu.VMEM((H_A, ROW_TILE, 2 * DK_A), BF16),
            pltpu.VMEM((H_A, ROW_TILE, DV_A), F32),
            pltpu.VMEM((ROW_TILE, d), F32),
        ],
        compiler_params=_params(2),
        name="hgrn_prompt",
    )(x, lb_raw, w_in, norm_g, w_out, ln_g, ln_b, masks, signs)


def _split3(f):
    hi = f.astype(BF16).astype(F32)
    r = f - hi
    mid = r.astype(BF16).astype(F32)
    lo = (r - mid).astype(BF16).astype(F32)
    return hi, mid, lo


def _hgrn_decode_kernel(layer, chained, x_ref, lbraw_ref, win_ref, ng_ref, wout_ref, lng_ref,
                        lnb_ref, s_ref, *rest):
    o_ref, snew_ref, f3_s, k_s, q_s, v_s, gate_s, oacc_s = rest[1:] if chained else rest

    @pl.when(pl.program_id(0) == 0)
    def _():
        _hgrn_decode_step(layer, x_ref, lbraw_ref, win_ref, ng_ref, wout_ref, lng_ref, lnb_ref,
                          s_ref, o_ref, snew_ref, f3_s, k_s, q_s, v_s, gate_s, oacc_s)

    @pl.when(pl.program_id(0) > 0)
    def _():
        snew_ref[...] = jnp.zeros_like(snew_ref)


def _hgrn_decode_step(layer, x_ref, lbraw_ref, win_ref, ng_ref, wout_ref, lng_ref, lnb_ref,
                      s_ref, o_ref, snew_ref, f3_s, k_s, q_s, v_s, gate_s, oacc_s):
    step = pl.program_id(1)
    d = D_MODEL
    nb = DEC_SEQS

    @pl.when(step == 0)
    def _():
        q, f, v, gate = _hgrn_gates(x_ref[...].astype(BF16), win_ref,
                                    _lower_bound(lbraw_ref[...], layer))
        hi, mid, lo = _split3(f)
        f3_s[0] = hi
        f3_s[1] = mid
        f3_s[2] = lo
        k_s[...] = 1.0 - f
        q_s[...] = q
        v_s[...] = v
        gate_s[...] = gate

    rows = pl.ds(pl.multiple_of(step * nb, nb), nb)
    fhi = f3_s[0, rows, :]
    fmid = f3_s[1, rows, :]
    flo = f3_s[2, rows, :]
    k8 = k_s[rows, :]
    q8 = q_s[rows, :]
    v8 = v_s[rows, :]
    ones = jnp.ones((nb, DV_A), F32)
    zeros = jnp.zeros((nb, DV_A), F32)
    rhs = []
    for h in range(H_A):
        ls = slice(h * DK_A, (h + 1) * DK_A)
        left = jnp.concatenate([ones, ones, ones, zeros], axis=0)
        right = jnp.concatenate([zeros, zeros, zeros, v8[:, ls]], axis=0)
        rhs.append(jnp.concatenate([left, right], axis=1).astype(BF16))
    seq = lax.broadcasted_iota(jnp.int32, (nb, DK_A), 0)
    head_lanes = [slice(h * DK_A, (h + 1) * DK_A) for h in range(H_A)]

    def group_body(p, o8):
        seqs = [p * DEC_SEQ_GROUP + i for i in range(DEC_SEQ_GROUP)]
        sels = [seq == i for i in seqs]
        chains = [(si, h) for si in range(DEC_SEQ_GROUP) for h in range(H_A)]
        s_old = [s_ref[seqs[si], h] for si, h in chains]
        lhs = [jnp.concatenate([jnp.where(sels[si], a[:, head_lanes[h]], 0.0)
                                for a in (fhi, fmid, flo, k8)], axis=0).astype(BF16)
               for si, h in chains]
        fb_kv = [_dot_tn(lhs[c], rhs[h]) for c, (si, h) in enumerate(chains)]
        s_new = [fb_kv[c][:, :DV_A] * s_old[c] + fb_kv[c][:, DV_A:] for c in range(len(chains))]
        qsel = [jnp.where(sels[si], q8[:, head_lanes[h]], 0.0).astype(BF16) for si, h in chains]
        outs = [_dot(qsel[c], s_new[c].astype(BF16)) for c in range(len(chains))]
        for c, (si, h) in enumerate(chains):
            snew_ref[seqs[si], h] = s_new[c]
        for si in range(DEC_SEQ_GROUP):
            o8 = o8 + jnp.concatenate(outs[si * H_A:(si + 1) * H_A], axis=1)
        return o8

    oacc_s[rows, :] = lax.fori_loop(0, nb // DEC_SEQ_GROUP, group_body, jnp.zeros((nb, d), F32))

    @pl.when(step == pl.num_programs(1) - 1)
    def _():
        o = oacc_s[...]
        norm_g = ng_ref[...]
        heads = []
        for h in range(H_A):
            oh = o[:, h * DV_A:(h + 1) * DV_A]
            ms = jnp.mean(oh * oh, axis=-1, keepdims=True)
            heads.append(oh * lax.rsqrt(ms + RMS_EPS) * norm_g)
        on = jnp.concatenate(heads, axis=1) * gate_s[...]
        y = _dot(on.astype(BF16), wout_ref[...])
        o_ref[...] = _layer_norm(ALPHA * x_ref[...] + y, lng_ref[...], lnb_ref[...])


def _hgrn_decode(x, layer, lb_raw, w_in, norm_g, w_out, ln_g, ln_b, states, new_states):
    n, d = x.shape
    j = layer // N_MIXERS
    assert n % DEC_SEQS == 0 and DEC_SEQS % DEC_SEQ_GROUP == 0
    chained = new_states is not None
    steps = n // DEC_SEQS
    passes = 1 if chained else states.shape[0] - j
    row_scratch = pltpu.VMEM((n, d), F32)
    state_block = (None, DEC_SEQS, H_A, DK_A, DV_A)
    state_in = pl.BlockSpec(state_block,
                            lambda g, s: (j, jnp.where(g == 0, s, steps - 1), 0, 0, 0))
    state_out = pl.BlockSpec(state_block, lambda g, s: (j + g, s, 0, 0, 0))
    operands = [x, lb_raw, w_in, norm_g, w_out, ln_g, ln_b, states]
    in_specs = [
        _const_spec(x.shape),
        _const_spec(lb_raw.shape),
        _layer_spec(w_in.shape, j),
        _layer_spec(norm_g.shape, j),
        _layer_spec(w_out.shape, j),
        _layer_spec(ln_g.shape, layer),
        _layer_spec(ln_b.shape, layer),
        state_in,
    ]
    if chained:
        operands.append(new_states)
        in_specs.append(pl.BlockSpec(memory_space=pl.ANY))
    return pl.pallas_call(
        functools.partial(_hgrn_decode_kernel, layer, chained),
        grid=(passes, steps),
        in_specs=in_specs,
        out_specs=[pl.BlockSpec((n, d), lambda g, s: (0, 0)), state_out],
        out_shape=[
            jax.ShapeDtypeStruct((n, d), F32),
            jax.ShapeDtypeStruct(states.shape, F32),
        ],
        scratch_shapes=[
            pltpu.VMEM((3, n, d), F32),
            row_scratch, row_scratch, row_scratch, row_scratch, row_scratch,
        ],
        input_output_aliases={len(operands) - 1: 1} if chained else {},
        compiler_params=_params(2),
        name="hgrn_decode",
    )(*operands)


def _cmlp_gate_values(x, win_ref, vg_ref, vb_ref):
    xb = x.astype(BF16)
    di = D_INNER_B
    u = _gelu(_dot(xb, win_ref[:, 0:di]))
    v = _gelu(_dot(xb, win_ref[:, di:2 * di]))
    return u, _layer_norm(v, vg_ref[...], vb_ref[...])


def _cmlp_finish(x, u, mixed, wout_ref, lng_ref, lnb_ref):
    y = _dot((u * mixed).astype(BF16), wout_ref[...])
    return _layer_norm(ALPHA * x + y, lng_ref[...], lnb_ref[...])


def _cmlp_kernel(x_ref, xdec_ref, win_ref, vg_ref, vb_ref, ws_ref, bias_ref, wout_ref,
                 lng_ref, lnb_ref, o_ref, vrows_ref, odec_ref, vdec_ref):
    x = x_ref[...]
    n = x.shape[0]
    u, v = _cmlp_gate_values(x, win_ref, vg_ref, vb_ref)
    row = lax.broadcasted_iota(jnp.int32, (CHUNK_B, CHUNK_B), 0)
    col = lax.broadcasted_iota(jnp.int32, (CHUNK_B, CHUNK_B), 1)
    w = [jnp.where(row >= col, ws_ref[g], 0.0).astype(BF16) for g in range(G_B)]
    bias = bias_ref[...]
    vb16 = v.astype(BF16)
    chunks = []
    for c in range(n // CHUNK_B):
        rs = slice(c * CHUNK_B, (c + 1) * CHUNK_B)
        groups = [_dot(w[g], vb16[rs, g * DG_B:(g + 1) * DG_B]) for g in range(G_B)]
        chunks.append(jnp.concatenate(groups, axis=1) + bias)
    o_ref[...] = _cmlp_finish(x, u, jnp.concatenate(chunks, axis=0), wout_ref, lng_ref, lnb_ref)

    @pl.when(pl.program_id(1) == pl.num_programs(1) - 1)
    def _():
        vrows_ref[...] = v[n - CHUNK_B:, :]

    @pl.when((pl.program_id(0) == 0) & (pl.program_id(1) == 0))
    def _():
        xd = xdec_ref[...]
        ud, vd = _cmlp_gate_values(xd, win_ref, vg_ref, vb_ref)
        w00 = jnp.concatenate([jnp.broadcast_to(ws_ref[g, 0:1, 0:1], (1, DG_B))
                               for g in range(G_B)], axis=1)
        vdec_ref[...] = vd
        odec_ref[...] = _cmlp_finish(xd, ud, vd * w00 + bias[0:1, :], wout_ref, lng_ref, lnb_ref)


def _cmlp(x, x_dec, j, layer, w_in, v_g, v_b, w_s, bias_full, w_out, ln_g, ln_b):
    b, l, d = x.shape
    assert l % ROW_TILE == 0 and ROW_TILE % CHUNK_B == 0
    dec_out = pl.BlockSpec(x_dec.shape, lambda i, t: (0, 0))
    return pl.pallas_call(
        _cmlp_kernel,
        grid=(b, l // ROW_TILE),
        in_specs=[
            pl.BlockSpec((None, ROW_TILE, d), lambda i, t: (i, t, 0)),
            _const_spec(x_dec.shape),
            _layer_spec(w_in.shape, j),
            _layer_spec(v_g.shape, j),
            _layer_spec(v_b.shape, j),
            _layer_spec(w_s.shape, j),
            _layer_spec(bias_full.shape, j),
            _layer_spec(w_out.shape, j),
            _layer_spec(ln_g.shape, layer),
            _layer_spec(ln_b.shape, layer),
        ],
        out_specs=[
            pl.BlockSpec((None, ROW_TILE, d), lambda i, t: (i, t, 0)),
            pl.BlockSpec((None, CHUNK_B, D_INNER_B), lambda i, t: (i, 0, 0)),
            dec_out,
            dec_out,
        ],
        out_shape=[
            jax.ShapeDtypeStruct((b, l, d), F32),
            jax.ShapeDtypeStruct((b, CHUNK_B, D_INNER_B), F32),
            jax.ShapeDtypeStruct(x_dec.shape, F32),
            jax.ShapeDtypeStruct(x_dec.shape, F32),
        ],
        compiler_params=_params(2),
        name="cmlp",
    )(x, x_dec, w_in, v_g, v_b, w_s, bias_full, w_out, ln_g, ln_b)


def _ffn_rows(x, win_ref, wout_ref, ln_g, ln_b):
    xb = x.astype(BF16)
    acc = jnp.zeros(x.shape, F32)
    for j in range(D_FF // FF_TILE):
        cs = slice(j * FF_TILE, (j + 1) * FF_TILE)
        us = slice(D_FF + j * FF_TILE, D_FF + (j + 1) * FF_TILE)
        hcol = jax.nn.silu(_dot(xb, win_ref[:, cs])) * _dot(xb, win_ref[:, us])
        acc = acc + _dot(hcol.astype(BF16), wout_ref[cs, :])
    return _layer_norm(ALPHA * x + acc, ln_g, ln_b)


def _ffn_kernel(x_ref, xdec_ref, win_ref, wout_ref, lng_ref, lnb_ref, o_ref, odec_ref):
    ln_g, ln_b = lng_ref[...], lnb_ref[...]
    o_ref[...] = _ffn_rows(x_ref[...], win_ref, wout_ref, ln_g, ln_b)

    @pl.when(pl.program_id(0) == 0)
    def _():
        odec_ref[...] = _ffn_rows(xdec_ref[...], win_ref, wout_ref, ln_g, ln_b)


def _ffn(x, x_dec, layer, w_in, w_out, ln_g, ln_b):
    n, d = x.shape
    assert n % FFN_ROW_TILE == 0 and D_FF % FF_TILE == 0
    return pl.pallas_call(
        _ffn_kernel,
        grid=(n // FFN_ROW_TILE,),
        in_specs=[
            pl.BlockSpec((FFN_ROW_TILE, d), lambda i: (i, 0)),
            _const_spec(x_dec.shape),
            _layer_spec(w_in.shape, layer),
            _layer_spec(w_out.shape, layer),
            _layer_spec(ln_g.shape, layer),
            _layer_spec(ln_b.shape, layer),
        ],
        out_specs=[
            pl.BlockSpec((FFN_ROW_TILE, d), lambda i: (i, 0)),
            pl.BlockSpec(x_dec.shape, lambda i: (0, 0)),
        ],
        out_shape=[
            jax.ShapeDtypeStruct((n, d), F32),
            jax.ShapeDtypeStruct(x_dec.shape, F32),
        ],
        compiler_params=_params(1),
        name="ffn",
    )(x, x_dec, w_in, w_out, ln_g, ln_b)


def kernel(x_prompt, x_sample, state_hgrn, ln_mix_g, ln_mix_b, ln_ffn_g, ln_ffn_b, a_lb_raw, a_w_in, a_norm_g, a_w_out, b_w_in, b_ln_g, b_ln_b, b_w_s, b_bias_s, b_w_out, ffn_w_in, ffn_w_out):
    bsz, seq, d = x_prompt.shape
    n_dec = x_sample.shape[0]
    assert x_sample.shape[1] == 1 and seq % CHUNK_B == 0

    masks = jnp.asarray(_level_masks(CHUNK_A))
    signs = jnp.asarray(_level_signs(CHUNK_A, DK_A))
    lb_raw = a_lb_raw.astype(F32)
    bias_full = jnp.repeat(jnp.swapaxes(b_bias_s, 1, 2), DG_B, axis=2)

    def rows(p):
        return p[:, None, :]

    mix_g, mix_b, ffn_g, ffn_b = rows(ln_mix_g), rows(ln_mix_b), rows(ln_ffn_g), rows(ln_ffn_b)
    norm_g, v_g, v_b = rows(a_norm_g), rows(b_ln_g), rows(b_ln_b)

    xp = x_prompt
    xs = x_sample.reshape(n_dec, d)
    st_p, v_p, v_s = [], [], []
    st_s = None
    for layer in range(DEPTH):
        j = layer // N_MIXERS
        if layer % N_MIXERS == 0:
            xp, s = _hgrn_prompt(xp, layer, lb_raw, a_w_in, norm_g, a_w_out, mix_g, mix_b,
                                 masks, signs)
            st_p.append(s)
            xs, st_s = _hgrn_decode(xs, layer, lb_raw, a_w_in, norm_g, a_w_out, mix_g, mix_b,
                                    state_hgrn, st_s)
        else:
            xp, v, xs, vd = _cmlp(xp, xs, j, layer, b_w_in, v_g, v_b, b_w_s, bias_full, b_w_out,
                                  mix_g, mix_b)
            v_p.append(v)
            v_s.append(vd[:, None, :])
        xp, xs = _ffn(xp.reshape(bsz * seq, d), xs, layer, ffn_w_in, ffn_w_out, ffn_g, ffn_b)
        xp = xp.reshape(bsz, seq, d)

    return (xp, xs.reshape(n_dec, 1, d), jnp.stack(st_p), st_s, jnp.stack(v_p), jnp.stack(v_s))
```

```python
import functools

import numpy as np
import jax
import jax.numpy as jnp
from jax import lax
from jax.experimental import pallas as pl
from jax.experimental.pallas import tpu as pltpu

D_MODEL = 1024
DEPTH = 4
N_MIXERS = 2
DK_A = 128
H_A = D_MODEL // DK_A
DV_A = D_MODEL // H_A
D_INNER_B = D_MODEL
CHUNK_B = 128
G_B = 8
DG_B = D_INNER_B // G_B
D_FF = -(-(8 * D_MODEL) // (3 * 256)) * 256
ALPHA = (2 * DEPTH) ** 0.25
LN_EPS = 1e-5
RMS_EPS = 1e-6

LANES = 128
SUBLANES = 8
MXU_DIM = 256
VMEM_LIMIT_BYTES = 56 * 2**20

ROW_TILE = 512
FFN_ROW_TILE = 1024
FFN_ROW_SPLIT = 2
CHUNK_A = 128
HEAD_GROUP = 8
STABLE_CHUNK_A = 64
DECAY_LIMIT_LOG2 = 118.0
STABLE_CHUNK_GROUP = 4
FINISH_ROWS_A = 256
FF_TILE = MXU_DIM
FFN_CAST_STEPS = 32
DEC_SEQS = 8
DEC_SEQ_GROUP = 4

BF16 = jnp.bfloat16
F32 = jnp.float32


def _level_halves(chunk):
    out, h = [], chunk // 2
    while h >= 1:
        out.append(h)
        h //= 2
    return tuple(out)


LEVEL_HALVES = _level_halves(CHUNK_A)


def _level_masks(chunk):
    t = np.arange(chunk)[:, None]
    s = np.arange(chunk)[None, :]
    masks = [((t // (2 * h) == s // (2 * h)) & (t % (2 * h) >= h) & (s % (2 * h) < h))
             for h in _level_halves(chunk)]
    masks.append(t == s)
    return np.stack(masks).astype(np.float32)


def _level_signs(chunk, width):
    t = np.arange(chunk)[:, None]
    signs = [np.where(t % (2 * h) >= h, 1.0, -1.0) for h in _level_halves(chunk)]
    return np.broadcast_to(np.stack(signs), (len(signs), chunk, width)).astype(np.float32)


def _dot(a, b):
    return lax.dot_general(a, b, (((1,), (0,)), ((), ())), preferred_element_type=F32)


def _dot_nt(a, b):
    return lax.dot_general(a, b, (((1,), (1,)), ((), ())), preferred_element_type=F32)


def _dot_tn(a, b):
    return lax.dot_general(a, b, (((0,), (0,)), ((), ())), preferred_element_type=F32)


def _layer_norm(z, g, b):
    mu = jnp.mean(z, axis=-1, keepdims=True)
    zc = z - mu
    var = jnp.mean(zc * zc, axis=-1, keepdims=True)
    return zc * lax.rsqrt(var + LN_EPS) * g + b


def _gelu(x):
    return 0.5 * x * (1.0 + lax.erf(x * (2.0 ** -0.5)))


def _lower_bound(raw, layer):
    m = jnp.max(raw, axis=0, keepdims=True)
    e = jnp.exp(raw - m)
    p = e / jnp.sum(e, axis=0, keepdims=True)
    c = p[0:1]
    for i in range(1, layer + 1):
        c = c + p[i:i + 1]
    return c - p[0:1]


def _hgrn_gates(xb, win_ref, lb):
    d = D_MODEL
    f = lb + (1.0 - lb) * jax.nn.sigmoid(_dot(xb, win_ref[:, d:2 * d]))
    q = jax.nn.silu(_dot(xb, win_ref[:, 0:d])) * (DK_A ** -0.5)
    gate = jax.nn.silu(_dot(xb, win_ref[:, 3 * d:4 * d]))
    v = _dot(xb, win_ref[:, 2 * d:3 * d])
    return q, f, v, gate


def _split2(a):
    hi = a.astype(BF16)
    lo = (a - hi.astype(F32)).astype(BF16)
    return jnp.concatenate([hi, lo], axis=1)


def _level_ref(g, half):
    n, w = g.shape
    if half >= SUBLANES // 2:
        blk = 2 * half
        parts = [jnp.broadcast_to(g[b * blk + half - 1:b * blk + half, :], (blk, w))
                 for b in range(n // blk)]
        return parts[0] if len(parts) == 1 else jnp.concatenate(parts, axis=0)
    row = lax.broadcasted_iota(jnp.int32, g.shape, 0)
    if half == 2:
        m = row % 4
        return jnp.where(m == 0, pltpu.roll(g, n - 1, axis=0),
                         jnp.where(m == 1, g,
                                   jnp.where(m == 2, pltpu.roll(g, 1, axis=0),
                                             pltpu.roll(g, 2, axis=0))))
    assert half == 1
    return jnp.where(row % 2 == 0, g, pltpu.roll(g, 1, axis=0))


def _rms_norm_rows(o, norm_g):
    ms = jnp.mean(o * o, axis=-1, keepdims=True)
    return o * lax.rsqrt(ms + RMS_EPS) * norm_g


def _hgrn_stable_chunks(heads, chunks, tri, causal, q_s, k_s, v_s, lf_s, oh_s, st_ref, norm_g):
    n_rows = STABLE_CHUNK_A
    items = [(c, h) for c in chunks for h in heads]
    n = range(len(items))
    rows = [slice(c * n_rows, (c + 1) * n_rows) for c, _ in items]
    q = [q_s[h, rows[i], :] for i, (_, h) in enumerate(items)]
    k = [k_s[h, rows[i], :] for i, (_, h) in enumerate(items)]
    vb = [v_s[h, rows[i], :] for i, (_, h) in enumerate(items)]
    g2 = [_dot(tri, lf_s[h, rows[i], :]) for i, (_, h) in enumerate(items)]
    g = [a[:, :DK_A] + a[:, DK_A:] for a in g2]
    g_last = [a[n_rows - 1:n_rows, :] for a in g]
    qp = [q[i] * jnp.exp2(g[i]).astype(BF16) for i in n]
    ks = [k[i] * jnp.exp2(g_last[i] - g[i]).astype(BF16) for i in n]
    kv = [_dot_tn(vb[i], ks[i]) for i in n]
    a = [causal * _dot_nt(qp[i], k[i] * jnp.exp2(-g[i]).astype(BF16)) for i in n]
    intra = [_dot(a[i].astype(BF16), vb[i]) for i in n]
    st = {h: st_ref[h] for h in heads}
    for i, (_, h) in enumerate(items):
        o = intra[i] + _dot_nt(qp[i], st[h].astype(BF16))
        oh_s[h, rows[i], :] = _rms_norm_rows(o, norm_g)
        st[h] = st[h] * jnp.exp2(g_last[i]) + kv[i]
    for h in heads:
        st_ref[h] = st[h]


def _hgrn_general_chunk(heads, c, tri, q_s, k_s, v_s, lf_s, oh_s, st_ref, mask_ref, sign_ref,
                        norm_g):
    rows = pl.ds(pl.multiple_of(c * CHUNK_A, CHUNK_A), CHUNK_A)
    n = range(len(heads))
    q = [q_s[h, rows, :] for h in heads]
    k = [k_s[h, rows, :] for h in heads]
    vb = [v_s[h, rows, :] for h in heads]
    st = [st_ref[h] for h in heads]
    g2 = [_dot(tri, lf_s[h, rows, :]) for h in heads]
    g = [a[:, :DK_A] + a[:, DK_A:] for a in g2]
    g_last = [a[CHUNK_A - 1:CHUNK_A, :] for a in g]
    o = [_dot_nt(q[i] * jnp.exp2(g[i]).astype(BF16), st[i].astype(BF16)) for i in n]
    ks = [k[i] * jnp.exp2(g_last[i] - g[i]).astype(BF16) for i in n]
    st_new = [st[i] * jnp.exp2(g_last[i]) + _dot_tn(vb[i], ks[i]) for i in n]
    diag = mask_ref[len(LEVEL_HALVES)]
    a = [diag * _dot_nt(q[i], k[i]) for i in n]
    for lvl, half in enumerate(LEVEL_HALVES):
        e = [jnp.exp2((g[i] - _level_ref(g[i], half)) * sign_ref[lvl]).astype(BF16) for i in n]
        p = [_dot_nt(q[i] * e[i], k[i] * e[i]) for i in n]
        a = [a[i] + mask_ref[lvl] * p[i] for i in n]
    o = [o[i] + _dot(a[i].astype(BF16), vb[i]) for i in n]
    for i, h in enumerate(heads):
        st_ref[h] = st_new[i]
        oh_s[h, rows, :] = _rms_norm_rows(o[i], norm_g)


def _hgrn_prompt_kernel(layer, x_ref, lbraw_ref, win_ref, ng_ref, wout_ref, lng_ref, lnb_ref,
                        mask_ref, sign_ref, fwi_ref, fwo_ref, o_ref, sfin_ref, fwib_ref, fwob_ref,
                        st_ref, q_s, k_s, v_s, lf_s, oh_s, gate_s):
    t = pl.program_id(1)
    _cast_ffn_slabs(fwi_ref, fwo_ref, fwib_ref, fwob_ref)

    @pl.when(t == 0)
    def _():
        st_ref[...] = jnp.zeros_like(st_ref)

    x = x_ref[0]
    q, f, v, gate = _hgrn_gates(x.astype(BF16), win_ref, _lower_bound(lbraw_ref[...], layer))
    k = 1.0 - f
    lf = jnp.log2(f)
    gate_s[...] = gate
    for h in range(H_A):
        ls = slice(h * DK_A, (h + 1) * DK_A)
        q_s[h] = q[:, ls].astype(BF16)
        k_s[h] = k[:, ls].astype(BF16)
        v_s[h] = v[:, ls].astype(BF16)
        lf_s[h] = _split2(lf[:, ls])

    norm_g = ng_ref[...]

    def lower_triangle(n_rows):
        row = lax.broadcasted_iota(jnp.int32, (n_rows, n_rows), 0)
        col = lax.broadcasted_iota(jnp.int32, (n_rows, n_rows), 1)
        return row >= col

    chunk_decay = jnp.sum(lf.reshape(ROW_TILE // STABLE_CHUNK_A, STABLE_CHUNK_A, D_MODEL), axis=1)
    stable = jnp.min(chunk_decay) >= -DECAY_LIMIT_LOG2

    def finish_rows(rows):
        o = jnp.concatenate([oh_s[h, rows, :] for h in range(H_A)], axis=1) * gate_s[rows, :]
        y = _dot(o.astype(BF16), wout_ref[...])
        o_ref[0, rows, :] = _layer_norm(ALPHA * x_ref[0, rows, :] + y, lng_ref[...], lnb_ref[...])

    def run_stable():
        causal = lower_triangle(STABLE_CHUNK_A)
        tri = causal.astype(BF16)
        causal = causal.astype(F32)
        heads = list(range(H_A))
        group = STABLE_CHUNK_GROUP
        for c0 in range(0, ROW_TILE // STABLE_CHUNK_A, group):
            _hgrn_stable_chunks(heads, list(range(c0, c0 + group)), tri, causal, q_s, k_s, v_s,
                                lf_s, oh_s, st_ref, norm_g)
            done = (c0 + group) * STABLE_CHUNK_A
            if done % FINISH_ROWS_A == 0:
                finish_rows(slice(done - FINISH_ROWS_A, done))

    def run_general():
        tri = lower_triangle(CHUNK_A).astype(BF16)

        def group_body(hg, carry):
            heads = [hg * HEAD_GROUP + i for i in range(HEAD_GROUP)]

            def chunk_body(c, carry2):
                _hgrn_general_chunk(heads, c, tri, q_s, k_s, v_s, lf_s, oh_s, st_ref, mask_ref,
                                    sign_ref, norm_g)
                return carry2
            return lax.fori_loop(0, ROW_TILE // CHUNK_A, chunk_body, carry)

        lax.fori_loop(0, H_A // HEAD_GROUP, group_body, 0)
        finish_rows(slice(0, ROW_TILE))

    lax.cond(stable, run_stable, run_general)

    @pl.when(t == pl.num_programs(1) - 1)
    def _():
        for h in range(H_A):
            sfin_ref[0, h] = st_ref[h].T


def _const_spec(shape):
    nd = len(shape)
    return pl.BlockSpec(shape, lambda *_: (0,) * nd, pipeline_mode=pl.Buffered(1))


def _layer_spec(shape, j):
    nd = len(shape)
    return pl.BlockSpec((None,) + tuple(shape[1:]), lambda *_: (j,) + (0,) * (nd - 1),
                        pipeline_mode=pl.Buffered(1))


def _ffn_cast_specs(ffn_w_in, ffn_w_out, layer, steps_per_row):
    _, d, ff2 = ffn_w_in.shape
    _, ff, _ = ffn_w_out.shape
    steps = FFN_CAST_STEPS
    rows_in, rows_out = d // steps, 2 * ff // steps
    assert d % steps == 0 and (2 * ff) % steps == 0 and rows_in % 16 == 0 and rows_out % 16 == 0

    def step(i, t):
        return i * steps_per_row + t

    in_specs = [
        pl.BlockSpec((None, rows_in, ff2), lambda i, t: (layer, step(i, t), 0)),
        pl.BlockSpec((None, rows_out, d), lambda i, t: (layer, step(i, t) // 2, 0)),
    ]
    out_specs = [
        pl.BlockSpec((rows_in, ff2), lambda i, t: (step(i, t), 0)),
        pl.BlockSpec((rows_out, d), lambda i, t: (step(i, t) // 2, 0)),
    ]
    out_shapes = [jax.ShapeDtypeStruct((d, ff2), BF16), jax.ShapeDtypeStruct((ff, d), BF16)]
    return in_specs, out_specs, out_shapes


def _cast_ffn_slabs(fwi_ref, fwo_ref, fwib_ref, fwob_ref):
    fwib_ref[...] = fwi_ref[...].astype(BF16)
    fwob_ref[...] = fwo_ref[...].astype(BF16)


def _params(n_axes):
    return pltpu.CompilerParams(dimension_semantics=("arbitrary",) * n_axes,
                                vmem_limit_bytes=VMEM_LIMIT_BYTES)


def _hgrn_prompt(x, layer, lb_raw, w_in, norm_g, w_out, ln_g, ln_b, masks, signs, ffn_w_in,
                 ffn_w_out):
    b, l, d = x.shape
    j = layer // N_MIXERS
    assert b * (l // ROW_TILE) == FFN_CAST_STEPS
    cast_in, cast_out, cast_shapes = _ffn_cast_specs(ffn_w_in, ffn_w_out, layer, l // ROW_TILE)
    assert l % ROW_TILE == 0 and ROW_TILE % CHUNK_A == 0 and H_A % HEAD_GROUP == 0
    assert ROW_TILE % FINISH_ROWS_A == 0
    assert FINISH_ROWS_A % (STABLE_CHUNK_GROUP * STABLE_CHUNK_A) == 0
    head_operand = pltpu.VMEM((H_A, ROW_TILE, DK_A), BF16)
    return pl.pallas_call(
        functools.partial(_hgrn_prompt_kernel, layer),
        grid=(b, l // ROW_TILE),
        in_specs=[
            pl.BlockSpec((1, ROW_TILE, d), lambda i, t: (i, t, 0)),
            _const_spec(lb_raw.shape),
            _layer_spec(w_in.shape, j),
            _layer_spec(norm_g.shape, j),
            _layer_spec(w_out.shape, j),
            _layer_spec(ln_g.shape, layer),
            _layer_spec(ln_b.shape, layer),
            _const_spec(masks.shape),
            _const_spec(signs.shape),
        ] + cast_in,
        out_specs=[
            pl.BlockSpec((1, ROW_TILE, d), lambda i, t: (i, t, 0)),
            pl.BlockSpec((1, H_A, DK_A, DV_A), lambda i, t: (i, 0, 0, 0)),
        ] + cast_out,
        out_shape=[
            jax.ShapeDtypeStruct((b, l, d), F32),
            jax.ShapeDtypeStruct((b, H_A, DK_A, DV_A), F32),
        ] + cast_shapes,
        scratch_shapes=[
            pltpu.VMEM((H_A, DV_A, DK_A), F32),
            head_operand, head_operand, head_operand,
            pltpu.VMEM((H_A, ROW_TILE, 2 * DK_A), BF16),
            pltpu.VMEM((H_A, ROW_TILE, DV_A), F32),
            pltpu.VMEM((ROW_TILE, d), F32),
        ],
        compiler_params=_params(2),
        name="hgrn_prompt",
    )(x, lb_raw, w_in, norm_g, w_out, ln_g, ln_b, masks, signs, ffn_w_in, ffn_w_out)


def _split3(f):
    hi = f.astype(BF16).astype(F32)
    r = f - hi
    mid = r.astype(BF16).astype(F32)
    lo = (r - mid).astype(BF16).astype(F32)
    return hi, mid, lo


def _hgrn_decode_kernel(layer, chained, x_ref, lbraw_ref, win_ref, ng_ref, wout_ref, lng_ref,
                        lnb_ref, s_ref, *rest):
    o_ref, snew_ref, f3_s, k_s, q_s, v_s, gate_s, oacc_s = rest[1:] if chained else rest

    @pl.when(pl.program_id(0) == 0)
    def _():
        _hgrn_decode_step(layer, x_ref, lbraw_ref, win_ref, ng_ref, wout_ref, lng_ref, lnb_ref,
                          s_ref, o_ref, snew_ref, f3_s, k_s, q_s, v_s, gate_s, oacc_s)

    @pl.when(pl.program_id(0) > 0)
    def _():
        snew_ref[...] = jnp.zeros_like(snew_ref)


def _hgrn_decode_step(layer, x_ref, lbraw_ref, win_ref, ng_ref, wout_ref, lng_ref, lnb_ref,
                      s_ref, o_ref, snew_ref, f3_s, k_s, q_s, v_s, gate_s, oacc_s):
    step = pl.program_id(1)
    d = D_MODEL
    nb = DEC_SEQS

    @pl.when(step == 0)
    def _():
        q, f, v, gate = _hgrn_gates(x_ref[...].astype(BF16), win_ref,
                                    _lower_bound(lbraw_ref[...], layer))
        hi, mid, lo = _split3(f)
        f3_s[0] = hi
        f3_s[1] = mid
        f3_s[2] = lo
        k_s[...] = 1.0 - f
        q_s[...] = q
        v_s[...] = v
        gate_s[...] = gate

    rows = pl.ds(pl.multiple_of(step * nb, nb), nb)
    fhi = f3_s[0, rows, :]
    fmid = f3_s[1, rows, :]
    flo = f3_s[2, rows, :]
    k8 = k_s[rows, :]
    q8 = q_s[rows, :]
    v8 = v_s[rows, :]
    ones = jnp.ones((nb, DV_A), F32)
    zeros = jnp.zeros((nb, DV_A), F32)
    rhs = []
    for h in range(H_A):
        ls = slice(h * DK_A, (h + 1) * DK_A)
        left = jnp.concatenate([ones, ones, ones, zeros], axis=0)
        right = jnp.concatenate([zeros, zeros, zeros, v8[:, ls]], axis=0)
        rhs.append(jnp.concatenate([left, right], axis=1).astype(BF16))
    seq = lax.broadcasted_iota(jnp.int32, (nb, DK_A), 0)
    head_lanes = [slice(h * DK_A, (h + 1) * DK_A) for h in range(H_A)]

    def group_body(p, o8):
        seqs = [p * DEC_SEQ_GROUP + i for i in range(DEC_SEQ_GROUP)]
        sels = [seq == i for i in seqs]
        chains = [(si, h) for si in range(DEC_SEQ_GROUP) for h in range(H_A)]
        s_old = [s_ref[seqs[si], h] for si, h in chains]
        lhs = [jnp.concatenate([jnp.where(sels[si], a[:, head_lanes[h]], 0.0)
                                for a in (fhi, fmid, flo, k8)], axis=0).astype(BF16)
               for si, h in chains]
        fb_kv = [_dot_tn(lhs[c], rhs[h]) for c, (si, h) in enumerate(chains)]
        s_new = [fb_kv[c][:, :DV_A] * s_old[c] + fb_kv[c][:, DV_A:] for c in range(len(chains))]
        qsel = [jnp.where(sels[si], q8[:, head_lanes[h]], 0.0).astype(BF16) for si, h in chains]
        outs = [_dot(qsel[c], s_new[c].astype(BF16)) for c in range(len(chains))]
        for c, (si, h) in enumerate(chains):
            snew_ref[seqs[si], h] = s_new[c]
        for si in range(DEC_SEQ_GROUP):
            o8 = o8 + jnp.concatenate(outs[si * H_A:(si + 1) * H_A], axis=1)
        return o8

    oacc_s[rows, :] = lax.fori_loop(0, nb // DEC_SEQ_GROUP, group_body, jnp.zeros((nb, d), F32))

    @pl.when(step == pl.num_programs(1) - 1)
    def _():
        o = oacc_s[...]
        norm_g = ng_ref[...]
        heads = []
        for h in range(H_A):
            oh = o[:, h * DV_A:(h + 1) * DV_A]
            ms = jnp.mean(oh * oh, axis=-1, keepdims=True)
            heads.append(oh * lax.rsqrt(ms + RMS_EPS) * norm_g)
        on = jnp.concatenate(heads, axis=1) * gate_s[...]
        y = _dot(on.astype(BF16), wout_ref[...])
        o_ref[...] = _layer_norm(ALPHA * x_ref[...] + y, lng_ref[...], lnb_ref[...])


def _hgrn_decode(x, layer, lb_raw, w_in, norm_g, w_out, ln_g, ln_b, states, new_states):
    n, d = x.shape
    j = layer // N_MIXERS
    assert n % DEC_SEQS == 0 and DEC_SEQS % DEC_SEQ_GROUP == 0
    chained = new_states is not None
    steps = n // DEC_SEQS
    passes = 1 if chained else states.shape[0] - j
    row_scratch = pltpu.VMEM((n, d), F32)
    state_block = (None, DEC_SEQS, H_A, DK_A, DV_A)
    state_in = pl.BlockSpec(state_block,
                            lambda g, s: (j, jnp.where(g == 0, s, steps - 1), 0, 0, 0))
    state_out = pl.BlockSpec(state_block, lambda g, s: (j + g, s, 0, 0, 0))
    operands = [x, lb_raw, w_in, norm_g, w_out, ln_g, ln_b, states]
    in_specs = [
        _const_spec(x.shape),
        _const_spec(lb_raw.shape),
        _layer_spec(w_in.shape, j),
        _layer_spec(norm_g.shape, j),
        _layer_spec(w_out.shape, j),
        _layer_spec(ln_g.shape, layer),
        _layer_spec(ln_b.shape, layer),
        state_in,
    ]
    if chained:
        operands.append(new_states)
        in_specs.append(pl.BlockSpec(memory_space=pl.ANY))
    return pl.pallas_call(
        functools.partial(_hgrn_decode_kernel, layer, chained),
        grid=(passes, steps),
        in_specs=in_specs,
        out_specs=[pl.BlockSpec((n, d), lambda g, s: (0, 0)), state_out],
        out_shape=[
            jax.ShapeDtypeStruct((n, d), F32),
            jax.ShapeDtypeStruct(states.shape, F32),
        ],
        scratch_shapes=[
            pltpu.VMEM((3, n, d), F32),
            row_scratch, row_scratch, row_scratch, row_scratch, row_scratch,
        ],
        input_output_aliases={len(operands) - 1: 1} if chained else {},
        compiler_params=_params(2),
        name="hgrn_decode",
    )(*operands)


def _cmlp_gate_values(x, win_ref, vg_ref, vb_ref):
    xb = x.astype(BF16)
    di = D_INNER_B
    v = _layer_norm(_gelu(_dot(xb, win_ref[:, di:2 * di])), vg_ref[...], vb_ref[...])
    u = _gelu(_dot(xb, win_ref[:, 0:di]))
    return u, v


def _cmlp_finish(x, u, mixed, wout_ref, lng_ref, lnb_ref):
    y = _dot((u * mixed).astype(BF16), wout_ref[...])
    return _layer_norm(ALPHA * x + y, lng_ref[...], lnb_ref[...])


def _cmlp_kernel(x_ref, xdec_ref, win_ref, vg_ref, vb_ref, ws_ref, bias_ref, wout_ref,
                 lng_ref, lnb_ref, fwi_ref, fwo_ref, o_ref, vrows_ref, odec_ref, vdec_ref,
                 fwib_ref, fwob_ref):
    _cast_ffn_slabs(fwi_ref, fwo_ref, fwib_ref, fwob_ref)
    x = x_ref[...]
    n = x.shape[0]
    u, v = _cmlp_gate_values(x, win_ref, vg_ref, vb_ref)
    row = lax.broadcasted_iota(jnp.int32, (CHUNK_B, CHUNK_B), 0)
    col = lax.broadcasted_iota(jnp.int32, (CHUNK_B, CHUNK_B), 1)
    w = [jnp.where(row >= col, ws_ref[g], 0.0).astype(BF16) for g in range(G_B)]
    bias = bias_ref[...]
    vb16 = v.astype(BF16)
    chunks = []
    for c in range(n // CHUNK_B):
        rs = slice(c * CHUNK_B, (c + 1) * CHUNK_B)
        groups = [_dot(w[g], vb16[rs, g * DG_B:(g + 1) * DG_B]) for g in range(G_B)]
        chunks.append(jnp.concatenate(groups, axis=1) + bias)
    o_ref[...] = _cmlp_finish(x, u, jnp.concatenate(chunks, axis=0), wout_ref, lng_ref, lnb_ref)

    @pl.when(pl.program_id(1) == pl.num_programs(1) - 1)
    def _():
        vrows_ref[...] = v[n - CHUNK_B:, :]

    @pl.when((pl.program_id(0) == 0) & (pl.program_id(1) == 0))
    def _():
        xd = xdec_ref[...]
        ud, vd = _cmlp_gate_values(xd, win_ref, vg_ref, vb_ref)
        w00 = jnp.concatenate([jnp.broadcast_to(ws_ref[g, 0:1, 0:1], (1, DG_B))
                               for g in range(G_B)], axis=1)
        vdec_ref[...] = vd
        odec_ref[...] = _cmlp_finish(xd, ud, vd * w00 + bias[0:1, :], wout_ref, lng_ref, lnb_ref)


def _cmlp(x, x_dec, j, layer, w_in, v_g, v_b, w_s, bias_full, w_out, ln_g, ln_b, ffn_w_in,
          ffn_w_out):
    b, l, d = x.shape
    assert l % ROW_TILE == 0 and ROW_TILE % CHUNK_B == 0
    assert b * (l // ROW_TILE) == FFN_CAST_STEPS
    cast_in, cast_out, cast_shapes = _ffn_cast_specs(ffn_w_in, ffn_w_out, layer, l // ROW_TILE)
    dec_out = pl.BlockSpec(x_dec.shape, lambda i, t: (0, 0))
    return pl.pallas_call(
        _cmlp_kernel,
        grid=(b, l // ROW_TILE),
        in_specs=[
            pl.BlockSpec((None, ROW_TILE, d), lambda i, t: (i, t, 0)),
            _const_spec(x_dec.shape),
            _layer_spec(w_in.shape, j),
            _layer_spec(v_g.shape, j),
            _layer_spec(v_b.shape, j),
            _layer_spec(w_s.shape, j),
            _layer_spec(bias_full.shape, j),
            _layer_spec(w_out.shape, j),
            _layer_spec(ln_g.shape, layer),
            _layer_spec(ln_b.shape, layer),
        ] + cast_in,
        out_specs=[
            pl.BlockSpec((None, ROW_TILE, d), lambda i, t: (i, t, 0)),
            pl.BlockSpec((None, CHUNK_B, D_INNER_B), lambda i, t: (i, 0, 0)),
            dec_out,
            dec_out,
        ] + cast_out,
        out_shape=[
            jax.ShapeDtypeStruct((b, l, d), F32),
            jax.ShapeDtypeStruct((b, CHUNK_B, D_INNER_B), F32),
            jax.ShapeDtypeStruct(x_dec.shape, F32),
            jax.ShapeDtypeStruct(x_dec.shape, F32),
        ] + cast_shapes,
        compiler_params=_params(2),
        name="cmlp",
    )(x, x_dec, w_in, v_g, v_b, w_s, bias_full, w_out, ln_g, ln_b, ffn_w_in, ffn_w_out)


def _ffn_rows(x, win_ref, wout_ref, ln_g, ln_b):
    xb = x.astype(BF16)
    acc = jnp.zeros(x.shape, F32)
    for j in range(D_FF // FF_TILE):
        cs = slice(j * FF_TILE, (j + 1) * FF_TILE)
        us = slice(D_FF + j * FF_TILE, D_FF + (j + 1) * FF_TILE)
        hcol = jax.nn.silu(_dot(xb, win_ref[:, cs])) * _dot(xb, win_ref[:, us])
        acc = acc + _dot(hcol.astype(BF16), wout_ref[cs, :])
    return _layer_norm(ALPHA * x + acc, ln_g, ln_b)


def _ffn_kernel(x_ref, xdec_ref, win_ref, wout_ref, lng_ref, lnb_ref, o_ref, odec_ref):
    ln_g, ln_b = lng_ref[...], lnb_ref[...]
    half = x_ref.shape[0] // FFN_ROW_SPLIT
    for i in range(FFN_ROW_SPLIT):
        rs = slice(i * half, (i + 1) * half)
        o_ref[rs, :] = _ffn_rows(x_ref[rs, :], win_ref, wout_ref, ln_g, ln_b)

    @pl.when(pl.program_id(0) == 0)
    def _():
        odec_ref[...] = _ffn_rows(xdec_ref[...], win_ref, wout_ref, ln_g, ln_b)


def _ffn(x, x_dec, layer, w_in, w_out, ln_g, ln_b):
    n, d = x.shape
    assert n % FFN_ROW_TILE == 0 and D_FF % FF_TILE == 0
    return pl.pallas_call(
        _ffn_kernel,
        grid=(n // FFN_ROW_TILE,),
        in_specs=[
            pl.BlockSpec((FFN_ROW_TILE, d), lambda i: (i, 0)),
            _const_spec(x_dec.shape),
            _const_spec(w_in.shape),
            _const_spec(w_out.shape),
            _layer_spec(ln_g.shape, layer),
            _layer_spec(ln_b.shape, layer),
        ],
        out_specs=[
            pl.BlockSpec((FFN_ROW_TILE, d), lambda i: (i, 0)),
            pl.BlockSpec(x_dec.shape, lambda i: (0, 0)),
        ],
        out_shape=[
            jax.ShapeDtypeStruct((n, d), F32),
            jax.ShapeDtypeStruct(x_dec.shape, F32),
        ],
        compiler_params=_params(1),
        name="ffn",
    )(x, x_dec, w_in, w_out, ln_g, ln_b)


def kernel(x_prompt, x_sample, state_hgrn, ln_mix_g, ln_mix_b, ln_ffn_g, ln_ffn_b, a_lb_raw, a_w_in, a_norm_g, a_w_out, b_w_in, b_ln_g, b_ln_b, b_w_s, b_bias_s, b_w_out, ffn_w_in, ffn_w_out):
    bsz, seq, d = x_prompt.shape
    n_dec = x_sample.shape[0]
    assert x_sample.shape[1] == 1 and seq % CHUNK_B == 0

    masks = jnp.asarray(_level_masks(CHUNK_A))
    signs = jnp.asarray(_level_signs(CHUNK_A, DK_A))
    lb_raw = a_lb_raw.astype(F32)
    bias_full = jnp.repeat(jnp.swapaxes(b_bias_s, 1, 2), DG_B, axis=2)

    def rows(p):
        return p[:, None, :]

    mix_g, mix_b, ffn_g, ffn_b = rows(ln_mix_g), rows(ln_mix_b), rows(ln_ffn_g), rows(ln_ffn_b)
    norm_g, v_g, v_b = rows(a_norm_g), rows(b_ln_g), rows(b_ln_b)

    xp = x_prompt
    xs = x_sample.reshape(n_dec, d)
    st_p, v_p, v_s = [], [], []
    st_s = None
    for layer in range(DEPTH):
        j = layer // N_MIXERS
        if layer % N_MIXERS == 0:
            xp, s, fwi, fwo = _hgrn_prompt(xp, layer, lb_raw, a_w_in, norm_g, a_w_out, mix_g,
                                           mix_b, masks, signs, ffn_w_in, ffn_w_out)
            st_p.append(s)
            xs, st_s = _hgrn_decode(xs, layer, lb_raw, a_w_in, norm_g, a_w_out, mix_g, mix_b,
                                    state_hgrn, st_s)
        else:
            xp, v, xs, vd, fwi, fwo = _cmlp(xp, xs, j, layer, b_w_in, v_g, v_b, b_w_s, bias_full,
                                            b_w_out, mix_g, mix_b, ffn_w_in, ffn_w_out)
            v_p.append(v)
            v_s.append(vd[:, None, :])
        xp, xs = _ffn(xp.reshape(bsz * seq, d), xs, layer, fwi, fwo, ffn_g, ffn_b)
        xp = xp.reshape(bsz, seq, d)

    return (xp, xs.reshape(n_dec, 1, d), jnp.stack(st_p), st_s, jnp.stack(v_p), jnp.stack(v_s))
```

```python
import functools

import numpy as np
import jax
import jax.numpy as jnp
from jax import lax
from jax.experimental import pallas as pl
from jax.experimental.pallas import tpu as pltpu

D_MODEL = 1024
DEPTH = 4
N_MIXERS = 2
DK_A = 128
H_A = D_MODEL // DK_A
DV_A = D_MODEL // H_A
D_INNER_B = D_MODEL
CHUNK_B = 128
G_B = 8
DG_B = D_INNER_B // G_B
D_FF = -(-(8 * D_MODEL) // (3 * 256)) * 256
ALPHA = (2 * DEPTH) ** 0.25
LN_EPS = 1e-5
RMS_EPS = 1e-6

LANES = 128
SUBLANES = 8
MXU_DIM = 256
VMEM_LIMIT_BYTES = 56 * 2**20

ROW_TILE = 512
FFN_ROW_TILE = 1024
FFN_ROW_SPLIT = 2
CHUNK_A = 128
HEAD_GROUP = 8
STABLE_CHUNK_A = 64
DECAY_LIMIT_LOG2 = 118.0
STABLE_CHUNK_GROUPS = (4, 4)
FF_TILE = MXU_DIM
FFN_CAST_STEPS = 32
DEC_SEQS = 8
DEC_SEQ_GROUP = 4

BF16 = jnp.bfloat16
F32 = jnp.float32


def _level_halves(chunk):
    out, h = [], chunk // 2
    while h >= 1:
        out.append(h)
        h //= 2
    return tuple(out)


LEVEL_HALVES = _level_halves(CHUNK_A)


def _level_masks(chunk):
    t = np.arange(chunk)[:, None]
    s = np.arange(chunk)[None, :]
    masks = [((t // (2 * h) == s // (2 * h)) & (t % (2 * h) >= h) & (s % (2 * h) < h))
             for h in _level_halves(chunk)]
    masks.append(t == s)
    return np.stack(masks).astype(np.float32)


def _level_signs(chunk, width):
    t = np.arange(chunk)[:, None]
    signs = [np.where(t % (2 * h) >= h, 1.0, -1.0) for h in _level_halves(chunk)]
    return np.broadcast_to(np.stack(signs), (len(signs), chunk, width)).astype(np.float32)


def _dot(a, b):
    return lax.dot_general(a, b, (((1,), (0,)), ((), ())), preferred_element_type=F32)


def _dot_nt(a, b):
    return lax.dot_general(a, b, (((1,), (1,)), ((), ())), preferred_element_type=F32)


def _dot_tn(a, b):
    return lax.dot_general(a, b, (((0,), (0,)), ((), ())), preferred_element_type=F32)


def _layer_norm(z, g, b):
    mu = jnp.mean(z, axis=-1, keepdims=True)
    zc = z - mu
    var = jnp.mean(zc * zc, axis=-1, keepdims=True)
    return zc * lax.rsqrt(var + LN_EPS) * g + b


def _gelu(x):
    return 0.5 * x * (1.0 + lax.erf(x * (2.0 ** -0.5)))


def _lower_bound(raw, layer):
    m = jnp.max(raw, axis=0, keepdims=True)
    e = jnp.exp(raw - m)
    p = e / jnp.sum(e, axis=0, keepdims=True)
    c = p[0:1]
    for i in range(1, layer + 1):
        c = c + p[i:i + 1]
    return c - p[0:1]


def _hgrn_gates(xb, win_ref, lb):
    d = D_MODEL
    f = lb + (1.0 - lb) * jax.nn.sigmoid(_dot(xb, win_ref[:, d:2 * d]))
    q = jax.nn.silu(_dot(xb, win_ref[:, 0:d])) * (DK_A ** -0.5)
    gate = jax.nn.silu(_dot(xb, win_ref[:, 3 * d:4 * d]))
    v = _dot(xb, win_ref[:, 2 * d:3 * d])
    return q, f, v, gate


def _split2(a):
    hi = a.astype(BF16)
    lo = (a - hi.astype(F32)).astype(BF16)
    return jnp.concatenate([hi, lo], axis=1)


def _level_ref(g, half):
    n, w = g.shape
    if half >= SUBLANES // 2:
        blk = 2 * half
        parts = [jnp.broadcast_to(g[b * blk + half - 1:b * blk + half, :], (blk, w))
                 for b in range(n // blk)]
        return parts[0] if len(parts) == 1 else jnp.concatenate(parts, axis=0)
    row = lax.broadcasted_iota(jnp.int32, g.shape, 0)
    if half == 2:
        m = row % 4
        return jnp.where(m == 0, pltpu.roll(g, n - 1, axis=0),
                         jnp.where(m == 1, g,
                                   jnp.where(m == 2, pltpu.roll(g, 1, axis=0),
                                             pltpu.roll(g, 2, axis=0))))
    assert half == 1
    return jnp.where(row % 2 == 0, g, pltpu.roll(g, 1, axis=0))


def _rms_norm_rows(o, norm_g):
    ms = jnp.mean(o * o, axis=-1, keepdims=True)
    return o * lax.rsqrt(ms + RMS_EPS) * norm_g


def _hgrn_stable_chunks(heads, chunks, tri, causal, q_s, k_s, v_s, lf_s, oh_s, st_ref, norm_g,
                        between):
    n_rows = STABLE_CHUNK_A
    items = [(c, h) for c in chunks for h in heads]
    n = range(len(items))
    rows = [slice(c * n_rows, (c + 1) * n_rows) for c, _ in items]
    q = [q_s[h, rows[i], :] for i, (_, h) in enumerate(items)]
    k = [k_s[h, rows[i], :] for i, (_, h) in enumerate(items)]
    vb = [v_s[h, rows[i], :] for i, (_, h) in enumerate(items)]
    g2 = [_dot(tri, _split2(lf_s[h, rows[i], :])) for i, (_, h) in enumerate(items)]
    between()
    g = [a[:, :DK_A] + a[:, DK_A:] for a in g2]
    g_last = [a[n_rows - 1:n_rows, :] for a in g]
    qp = [q[i] * jnp.exp2(g[i]).astype(BF16) for i in n]
    ks = [k[i] * jnp.exp2(g_last[i] - g[i]).astype(BF16) for i in n]
    kv = [_dot_tn(vb[i], ks[i]) for i in n]
    a = [causal * _dot_nt(qp[i], k[i] * jnp.exp2(-g[i]).astype(BF16)) for i in n]
    between()
    intra = [_dot(a[i].astype(BF16), vb[i]) for i in n]
    between()
    st = {h: st_ref[h] for h in heads}
    for i, (_, h) in enumerate(items):
        o = intra[i] + _dot_nt(qp[i], st[h].astype(BF16))
        oh_s[h, rows[i], :] = _rms_norm_rows(o, norm_g)
        st[h] = st[h] * jnp.exp2(g_last[i]) + kv[i]
    for h in heads:
        st_ref[h] = st[h]


def _hgrn_general_chunk(heads, c, tri, q_s, k_s, v_s, lf_s, oh_s, st_ref, mask_ref, sign_ref,
                        norm_g):
    rows = pl.ds(pl.multiple_of(c * CHUNK_A, CHUNK_A), CHUNK_A)
    n = range(len(heads))
    q = [q_s[h, rows, :] for h in heads]
    k = [k_s[h, rows, :] for h in heads]
    vb = [v_s[h, rows, :] for h in heads]
    st = [st_ref[h] for h in heads]
    g2 = [_dot(tri, _split2(lf_s[h, rows, :])) for h in heads]
    g = [a[:, :DK_A] + a[:, DK_A:] for a in g2]
    g_last = [a[CHUNK_A - 1:CHUNK_A, :] for a in g]
    o = [_dot_nt(q[i] * jnp.exp2(g[i]).astype(BF16), st[i].astype(BF16)) for i in n]
    ks = [k[i] * jnp.exp2(g_last[i] - g[i]).astype(BF16) for i in n]
    st_new = [st[i] * jnp.exp2(g_last[i]) + _dot_tn(vb[i], ks[i]) for i in n]
    diag = mask_ref[len(LEVEL_HALVES)]
    a = [diag * _dot_nt(q[i], k[i]) for i in n]
    for lvl, half in enumerate(LEVEL_HALVES):
        e = [jnp.exp2((g[i] - _level_ref(g[i], half)) * sign_ref[lvl]).astype(BF16) for i in n]
        p = [_dot_nt(q[i] * e[i], k[i] * e[i]) for i in n]
        a = [a[i] + mask_ref[lvl] * p[i] for i in n]
    o = [o[i] + _dot(a[i].astype(BF16), vb[i]) for i in n]
    for i, h in enumerate(heads):
        st_ref[h] = st_new[i]
        oh_s[h, rows, :] = _rms_norm_rows(o[i], norm_g)


PROJ_TILES = [(section, tile) for section in (1, 0, 3, 2) for tile in range(D_MODEL // MXU_DIM)]
HEADS_PER_TILE = MXU_DIM // DK_A
PROJ_TILES_PER_SLOT = (4, 2, 1, 1)


def _hgrn_project_tile(section, tile, xb, win_ref, lb):
    c0 = section * D_MODEL + tile * MXU_DIM
    a = _dot(xb, win_ref[:, c0:c0 + MXU_DIM])
    if section == 0:
        return (jax.nn.silu(a) * (DK_A ** -0.5)).astype(BF16)
    if section == 1:
        lbt = lb[:, tile * MXU_DIM:(tile + 1) * MXU_DIM]
        f = lbt + (1.0 - lbt) * jax.nn.sigmoid(a)
        lf = jnp.log2(f)
        n_chunks = lf.shape[0] // STABLE_CHUNK_A
        decay = jnp.min(jnp.sum(lf.reshape(n_chunks, STABLE_CHUNK_A, MXU_DIM), axis=1))
        return (1.0 - f).astype(BF16), lf, decay
    if section == 2:
        return a.astype(BF16)
    return jax.nn.silu(a)


def _hgrn_store_projection(pieces, q_s, k_s, v_s, lf_s, gate_s, stable_s):
    decay = None
    for tile in range(D_MODEL // MXU_DIM):
        k, lf, tile_decay = pieces[(1, tile)]
        decay = tile_decay if decay is None else jnp.minimum(decay, tile_decay)
        gate_s[:, tile * MXU_DIM:(tile + 1) * MXU_DIM] = pieces[(3, tile)]
        for i in range(HEADS_PER_TILE):
            h = tile * HEADS_PER_TILE + i
            ls = slice(i * DK_A, (i + 1) * DK_A)
            q_s[h] = pieces[(0, tile)][:, ls]
            k_s[h] = k[:, ls]
            v_s[h] = pieces[(2, tile)][:, ls]
            lf_s[h] = lf[:, ls]
    stable_s[0] = (decay >= -DECAY_LIMIT_LOG2).astype(jnp.int32)


def _hgrn_prompt_kernel(layer, x_ref, xnext_ref, lbraw_ref, win_ref, ng_ref, wout_ref, lng_ref,
                        lnb_ref, mask_ref, sign_ref, fwi_ref, fwo_ref, o_ref, sfin_ref, fwib_ref,
                        fwob_ref, st_ref, q_s, k_s, v_s, lf_s, oh_s, gate_s, stable_s):
    t = pl.program_id(1)
    _cast_ffn_slabs(fwi_ref, fwo_ref, fwib_ref, fwob_ref)
    lb = _lower_bound(lbraw_ref[...], layer)

    def project(x):
        xb = x.astype(BF16)
        return {st: _hgrn_project_tile(st[0], st[1], xb, win_ref, lb) for st in PROJ_TILES}

    def store(pieces):
        _hgrn_store_projection(pieces, q_s, k_s, v_s, lf_s, gate_s, stable_s)

    @pl.when((pl.program_id(0) == 0) & (t == 0))
    def _():
        store(project(x_ref[0]))

    @pl.when(t == 0)
    def _():
        st_ref[...] = jnp.zeros_like(st_ref)

    norm_g = ng_ref[...]

    def lower_triangle(n_rows):
        row = lax.broadcasted_iota(jnp.int32, (n_rows, n_rows), 0)
        col = lax.broadcasted_iota(jnp.int32, (n_rows, n_rows), 1)
        return row >= col

    def finish_rows(rows):
        o = jnp.concatenate([oh_s[h, rows, :] for h in range(H_A)], axis=1) * gate_s[rows, :]
        y = _dot(o.astype(BF16), wout_ref[...])
        o_ref[0, rows, :] = _layer_norm(ALPHA * x_ref[0, rows, :] + y, lng_ref[...], lnb_ref[...])

    def run_stable():
        causal = lower_triangle(STABLE_CHUNK_A)
        tri = causal.astype(BF16)
        causal = causal.astype(F32)
        heads = list(range(H_A))
        xb = xnext_ref[0].astype(BF16)
        todo = list(PROJ_TILES)
        pieces = {}
        share = iter(PROJ_TILES_PER_SLOT * len(STABLE_CHUNK_GROUPS))

        def between():
            for st in todo[:next(share)]:
                pieces[st] = _hgrn_project_tile(st[0], st[1], xb, win_ref, lb)
                todo.remove(st)

        c0 = 0
        for group in STABLE_CHUNK_GROUPS:
            _hgrn_stable_chunks(heads, list(range(c0, c0 + group)), tri, causal, q_s, k_s, v_s,
                                lf_s, oh_s, st_ref, norm_g, between)
            finish_rows(slice(c0 * STABLE_CHUNK_A, (c0 + group) * STABLE_CHUNK_A))
            between()
            c0 += group
        assert not todo
        store(pieces)

    def run_general():
        tri = lower_triangle(CHUNK_A).astype(BF16)

        def group_body(hg, carry):
            heads = [hg * HEAD_GROUP + i for i in range(HEAD_GROUP)]

            def chunk_body(c, carry2):
                _hgrn_general_chunk(heads, c, tri, q_s, k_s, v_s, lf_s, oh_s, st_ref, mask_ref,
                                    sign_ref, norm_g)
                return carry2
            return lax.fori_loop(0, ROW_TILE // CHUNK_A, chunk_body, carry)

        lax.fori_loop(0, H_A // HEAD_GROUP, group_body, 0)
        finish_rows(slice(0, ROW_TILE))
        store(project(xnext_ref[0]))

    lax.cond(stable_s[0] != 0, run_stable, run_general)

    @pl.when(t == pl.num_programs(1) - 1)
    def _():
        for h in range(H_A):
            sfin_ref[0, h] = st_ref[h].T


def _const_spec(shape):
    nd = len(shape)
    return pl.BlockSpec(shape, lambda *_: (0,) * nd, pipeline_mode=pl.Buffered(1))


def _layer_spec(shape, j):
    nd = len(shape)
    return pl.BlockSpec((None,) + tuple(shape[1:]), lambda *_: (j,) + (0,) * (nd - 1),
                        pipeline_mode=pl.Buffered(1))


def _ffn_cast_specs(ffn_w_in, ffn_w_out, layer, steps_per_row):
    _, d, ff2 = ffn_w_in.shape
    _, ff, _ = ffn_w_out.shape
    steps = FFN_CAST_STEPS
    rows_in, rows_out = d // steps, 2 * ff // steps
    assert d % steps == 0 and (2 * ff) % steps == 0 and rows_in % 16 == 0 and rows_out % 16 == 0

    def step(i, t):
        return i * steps_per_row + t

    in_specs = [
        pl.BlockSpec((None, rows_in, ff2), lambda i, t: (layer, step(i, t), 0)),
        pl.BlockSpec((None, rows_out, d), lambda i, t: (layer, step(i, t) // 2, 0)),
    ]
    out_specs = [
        pl.BlockSpec((rows_in, ff2), lambda i, t: (step(i, t), 0)),
        pl.BlockSpec((rows_out, d), lambda i, t: (step(i, t) // 2, 0)),
    ]
    out_shapes = [jax.ShapeDtypeStruct((d, ff2), BF16), jax.ShapeDtypeStruct((ff, d), BF16)]
    return in_specs, out_specs, out_shapes


def _cast_ffn_slabs(fwi_ref, fwo_ref, fwib_ref, fwob_ref):
    fwib_ref[...] = fwi_ref[...].astype(BF16)
    fwob_ref[...] = fwo_ref[...].astype(BF16)


def _params(n_axes):
    return pltpu.CompilerParams(dimension_semantics=("arbitrary",) * n_axes,
                                vmem_limit_bytes=VMEM_LIMIT_BYTES)


def _hgrn_prompt(x, layer, lb_raw, w_in, norm_g, w_out, ln_g, ln_b, masks, signs, ffn_w_in,
                 ffn_w_out):
    b, l, d = x.shape
    j = layer // N_MIXERS
    assert b * (l // ROW_TILE) == FFN_CAST_STEPS
    cast_in, cast_out, cast_shapes = _ffn_cast_specs(ffn_w_in, ffn_w_out, layer, l // ROW_TILE)
    assert l % ROW_TILE == 0 and ROW_TILE % CHUNK_A == 0 and H_A % HEAD_GROUP == 0
    assert sum(STABLE_CHUNK_GROUPS) * STABLE_CHUNK_A == ROW_TILE
    nt = l // ROW_TILE

    def next_tile(i, t):
        last = (i == b - 1) & (t == nt - 1)
        wrap = t == nt - 1
        return (jnp.where(wrap & ~last, i + 1, i), jnp.where(last, t, jnp.where(wrap, 0, t + 1)), 0)

    head_operand = pltpu.VMEM((H_A, ROW_TILE, DK_A), BF16)
    return pl.pallas_call(
        functools.partial(_hgrn_prompt_kernel, layer),
        grid=(b, l // ROW_TILE),
        in_specs=[
            pl.BlockSpec((1, ROW_TILE, d), lambda i, t: (i, t, 0)),
            pl.BlockSpec((1, ROW_TILE, d), next_tile),
            _const_spec(lb_raw.shape),
            _layer_spec(w_in.shape, j),
            _layer_spec(norm_g.shape, j),
            _layer_spec(w_out.shape, j),
            _layer_spec(ln_g.shape, layer),
            _layer_spec(ln_b.shape, layer),
            _const_spec(masks.shape),
            _const_spec(signs.shape),
        ] + cast_in,
        out_specs=[
            pl.BlockSpec((1, ROW_TILE, d), lambda i, t: (i, t, 0)),
            pl.BlockSpec((1, H_A, DK_A, DV_A), lambda i, t: (i, 0, 0, 0)),
        ] + cast_out,
        out_shape=[
            jax.ShapeDtypeStruct((b, l, d), F32),
            jax.ShapeDtypeStruct((b, H_A, DK_A, DV_A), F32),
        ] + cast_shapes,
        scratch_shapes=[
            pltpu.VMEM((H_A, DV_A, DK_A), F32),
            head_operand, head_operand, head_operand,
            pltpu.VMEM((H_A, ROW_TILE, DK_A), F32),
            pltpu.VMEM((H_A, ROW_TILE, DV_A), F32),
            pltpu.VMEM((ROW_TILE, d), F32),
            pltpu.SMEM((1,), jnp.int32),
        ],
        compiler_params=_params(2),
        name="hgrn_prompt",
    )(x, x, lb_raw, w_in, norm_g, w_out, ln_g, ln_b, masks, signs, ffn_w_in, ffn_w_out)


def _split3(f):
    hi = f.astype(BF16).astype(F32)
    r = f - hi
    mid = r.astype(BF16).astype(F32)
    lo = (r - mid).astype(BF16).astype(F32)
    return hi, mid, lo


def _hgrn_decode_kernel(layer, chained, x_ref, lbraw_ref, win_ref, ng_ref, wout_ref, lng_ref,
                        lnb_ref, s_ref, *rest):
    o_ref, snew_ref, f3_s, k_s, q_s, v_s, gate_s, oacc_s = rest[1:] if chained else rest

    @pl.when(pl.program_id(0) == 0)
    def _():
        _hgrn_decode_step(layer, x_ref, lbraw_ref, win_ref, ng_ref, wout_ref, lng_ref, lnb_ref,
                          s_ref, o_ref, snew_ref, f3_s, k_s, q_s, v_s, gate_s, oacc_s)

    @pl.when(pl.program_id(0) > 0)
    def _():
        snew_ref[...] = jnp.zeros_like(snew_ref)


def _hgrn_decode_step(layer, x_ref, lbraw_ref, win_ref, ng_ref, wout_ref, lng_ref, lnb_ref,
                      s_ref, o_ref, snew_ref, f3_s, k_s, q_s, v_s, gate_s, oacc_s):
    step = pl.program_id(1)
    d = D_MODEL
    nb = DEC_SEQS

    @pl.when(step == 0)
    def _():
        q, f, v, gate = _hgrn_gates(x_ref[...].astype(BF16), win_ref,
                                    _lower_bound(lbraw_ref[...], layer))
        hi, mid, lo = _split3(f)
        f3_s[0] = hi
        f3_s[1] = mid
        f3_s[2] = lo
        k_s[...] = 1.0 - f
        q_s[...] = q
        v_s[...] = v
        gate_s[...] = gate

    rows = pl.ds(pl.multiple_of(step * nb, nb), nb)
    fhi = f3_s[0, rows, :]
    fmid = f3_s[1, rows, :]
    flo = f3_s[2, rows, :]
    k8 = k_s[rows, :]
    q8 = q_s[rows, :]
    v8 = v_s[rows, :]
    ones = jnp.ones((nb, DV_A), F32)
    zeros = jnp.zeros((nb, DV_A), F32)
    rhs = []
    for h in range(H_A):
        ls = slice(h * DK_A, (h + 1) * DK_A)
        left = jnp.concatenate([ones, ones, ones, zeros], axis=0)
        right = jnp.concatenate([zeros, zeros, zeros, v8[:, ls]], axis=0)
        rhs.append(jnp.concatenate([left, right], axis=1).astype(BF16))
    seq = lax.broadcasted_iota(jnp.int32, (nb, DK_A), 0)
    head_lanes = [slice(h * DK_A, (h + 1) * DK_A) for h in range(H_A)]

    def group_body(p, o8):
        seqs = [p * DEC_SEQ_GROUP + i for i in range(DEC_SEQ_GROUP)]
        sels = [seq == i for i in seqs]
        chains = [(si, h) for si in range(DEC_SEQ_GROUP) for h in range(H_A)]
        s_old = [s_ref[seqs[si], h] for si, h in chains]
        lhs = [jnp.concatenate([jnp.where(sels[si], a[:, head_lanes[h]], 0.0)
                                for a in (fhi, fmid, flo, k8)], axis=0).astype(BF16)
               for si, h in chains]
        fb_kv = [_dot_tn(lhs[c], rhs[h]) for c, (si, h) in enumerate(chains)]
        s_new = [fb_kv[c][:, :DV_A] * s_old[c] + fb_kv[c][:, DV_A:] for c in range(len(chains))]
        qsel = [jnp.where(sels[si], q8[:, head_lanes[h]], 0.0).astype(BF16) for si, h in chains]
        outs = [_dot(qsel[c], s_new[c].astype(BF16)) for c in range(len(chains))]
        for c, (si, h) in enumerate(chains):
            snew_ref[seqs[si], h] = s_new[c]
        for si in range(DEC_SEQ_GROUP):
            o8 = o8 + jnp.concatenate(outs[si * H_A:(si + 1) * H_A], axis=1)
        return o8

    oacc_s[rows, :] = lax.fori_loop(0, nb // DEC_SEQ_GROUP, group_body, jnp.zeros((nb, d), F32))

    @pl.when(step == pl.num_programs(1) - 1)
    def _():
        o = oacc_s[...]
        norm_g = ng_ref[...]
        heads = []
        for h in range(H_A):
            oh = o[:, h * DV_A:(h + 1) * DV_A]
            ms = jnp.mean(oh * oh, axis=-1, keepdims=True)
            heads.append(oh * lax.rsqrt(ms + RMS_EPS) * norm_g)
        on = jnp.concatenate(heads, axis=1) * gate_s[...]
        y = _dot(on.astype(BF16), wout_ref[...])
        o_ref[...] = _layer_norm(ALPHA * x_ref[...] + y, lng_ref[...], lnb_ref[...])


def _hgrn_decode(x, layer, lb_raw, w_in, norm_g, w_out, ln_g, ln_b, states, new_states):
    n, d = x.shape
    j = layer // N_MIXERS
    assert n % DEC_SEQS == 0 and DEC_SEQS % DEC_SEQ_GROUP == 0
    chained = new_states is not None
    steps = n // DEC_SEQS
    passes = 1 if chained else states.shape[0] - j
    row_scratch = pltpu.VMEM((n, d), F32)
    state_block = (None, DEC_SEQS, H_A, DK_A, DV_A)
    state_in = pl.BlockSpec(state_block,
                            lambda g, s: (j, jnp.where(g == 0, s, steps - 1), 0, 0, 0))
    state_out = pl.BlockSpec(state_block, lambda g, s: (j + g, s, 0, 0, 0))
    operands = [x, lb_raw, w_in, norm_g, w_out, ln_g, ln_b, states]
    in_specs = [
        _const_spec(x.shape),
        _const_spec(lb_raw.shape),
        _layer_spec(w_in.shape, j),
        _layer_spec(norm_g.shape, j),
        _layer_spec(w_out.shape, j),
        _layer_spec(ln_g.shape, layer),
        _layer_spec(ln_b.shape, layer),
        state_in,
    ]
    if chained:
        operands.append(new_states)
        in_specs.append(pl.BlockSpec(memory_space=pl.ANY))
    return pl.pallas_call(
        functools.partial(_hgrn_decode_kernel, layer, chained),
        grid=(passes, steps),
        in_specs=in_specs,
        out_specs=[pl.BlockSpec((n, d), lambda g, s: (0, 0)), state_out],
        out_shape=[
            jax.ShapeDtypeStruct((n, d), F32),
            jax.ShapeDtypeStruct(states.shape, F32),
        ],
        scratch_shapes=[
            pltpu.VMEM((3, n, d), F32),
            row_scratch, row_scratch, row_scratch, row_scratch, row_scratch,
        ],
        input_output_aliases={len(operands) - 1: 1} if chained else {},
        compiler_params=_params(2),
        name="hgrn_decode",
    )(*operands)


def _cmlp_gate_values(x, win_ref, vg_ref, vb_ref):
    xb = x.astype(BF16)
    di = D_INNER_B
    v = _layer_norm(_gelu(_dot(xb, win_ref[:, di:2 * di])), vg_ref[...], vb_ref[...])
    u = _gelu(_dot(xb, win_ref[:, 0:di]))
    return u, v


def _cmlp_finish(x, u, mixed, wout_ref, lng_ref, lnb_ref):
    y = _dot((u * mixed).astype(BF16), wout_ref[...])
    return _layer_norm(ALPHA * x + y, lng_ref[...], lnb_ref[...])


def _cmlp_kernel(x_ref, xdec_ref, win_ref, vg_ref, vb_ref, ws_ref, bias_ref, wout_ref,
                 lng_ref, lnb_ref, fwi_ref, fwo_ref, o_ref, vrows_ref, odec_ref, vdec_ref,
                 fwib_ref, fwob_ref):
    _cast_ffn_slabs(fwi_ref, fwo_ref, fwib_ref, fwob_ref)
    x = x_ref[...]
    n = x.shape[0]
    u, v = _cmlp_gate_values(x, win_ref, vg_ref, vb_ref)
    row = lax.broadcasted_iota(jnp.int32, (CHUNK_B, CHUNK_B), 0)
    col = lax.broadcasted_iota(jnp.int32, (CHUNK_B, CHUNK_B), 1)
    w = [jnp.where(row >= col, ws_ref[g], 0.0).astype(BF16) for g in range(G_B)]
    bias = bias_ref[...]
    vb16 = v.astype(BF16)
    chunks = []
    for c in range(n // CHUNK_B):
        rs = slice(c * CHUNK_B, (c + 1) * CHUNK_B)
        groups = [_dot(w[g], vb16[rs, g * DG_B:(g + 1) * DG_B]) for g in range(G_B)]
        chunks.append(jnp.concatenate(groups, axis=1) + bias)
    o_ref[...] = _cmlp_finish(x, u, jnp.concatenate(chunks, axis=0), wout_ref, lng_ref, lnb_ref)

    @pl.when(pl.program_id(1) == pl.num_programs(1) - 1)
    def _():
        vrows_ref[...] = v[n - CHUNK_B:, :]

    @pl.when((pl.program_id(0) == 0) & (pl.program_id(1) == 0))
    def _():
        xd = xdec_ref[...]
        ud, vd = _cmlp_gate_values(xd, win_ref, vg_ref, vb_ref)
        w00 = jnp.concatenate([jnp.broadcast_to(ws_ref[g, 0:1, 0:1], (1, DG_B))
                               for g in range(G_B)], axis=1)
        vdec_ref[...] = vd
        odec_ref[...] = _cmlp_finish(xd, ud, vd * w00 + bias[0:1, :], wout_ref, lng_ref, lnb_ref)


def _cmlp(x, x_dec, j, layer, w_in, v_g, v_b, w_s, bias_full, w_out, ln_g, ln_b, ffn_w_in,
          ffn_w_out):
    b, l, d = x.shape
    assert l % ROW_TILE == 0 and ROW_TILE % CHUNK_B == 0
    assert b * (l // ROW_TILE) == FFN_CAST_STEPS
    cast_in, cast_out, cast_shapes = _ffn_cast_specs(ffn_w_in, ffn_w_out, layer, l // ROW_TILE)
    dec_out = pl.BlockSpec(x_dec.shape, lambda i, t: (0, 0))
    return pl.pallas_call(
        _cmlp_kernel,
        grid=(b, l // ROW_TILE),
        in_specs=[
            pl.BlockSpec((None, ROW_TILE, d), lambda i, t: (i, t, 0)),
            _const_spec(x_dec.shape),
            _layer_spec(w_in.shape, j),
            _layer_spec(v_g.shape, j),
            _layer_spec(v_b.shape, j),
            _layer_spec(w_s.shape, j),
            _layer_spec(bias_full.shape, j),
            _layer_spec(w_out.shape, j),
            _layer_spec(ln_g.shape, layer),
            _layer_spec(ln_b.shape, layer),
        ] + cast_in,
        out_specs=[
            pl.BlockSpec((None, ROW_TILE, d), lambda i, t: (i, t, 0)),
            pl.BlockSpec((None, CHUNK_B, D_INNER_B), lambda i, t: (i, 0, 0)),
            dec_out,
            dec_out,
        ] + cast_out,
        out_shape=[
            jax.ShapeDtypeStruct((b, l, d), F32),
            jax.ShapeDtypeStruct((b, CHUNK_B, D_INNER_B), F32),
            jax.ShapeDtypeStruct(x_dec.shape, F32),
            jax.ShapeDtypeStruct(x_dec.shape, F32),
        ] + cast_shapes,
        compiler_params=_params(2),
        name="cmlp",
    )(x, x_dec, w_in, v_g, v_b, w_s, bias_full, w_out, ln_g, ln_b, ffn_w_in, ffn_w_out)


def _ffn_rows(x, win_ref, wout_ref, ln_g, ln_b):
    xb = x.astype(BF16)
    acc = jnp.zeros(x.shape, F32)
    for j in range(D_FF // FF_TILE):
        cs = slice(j * FF_TILE, (j + 1) * FF_TILE)
        us = slice(D_FF + j * FF_TILE, D_FF + (j + 1) * FF_TILE)
        hcol = jax.nn.silu(_dot(xb, win_ref[:, cs])) * _dot(xb, win_ref[:, us])
        acc = acc + _dot(hcol.astype(BF16), wout_ref[cs, :])
    return _layer_norm(ALPHA * x + acc, ln_g, ln_b)


def _ffn_kernel(x_ref, xdec_ref, win_ref, wout_ref, lng_ref, lnb_ref, o_ref, odec_ref):
    ln_g, ln_b = lng_ref[...], lnb_ref[...]
    half = x_ref.shape[0] // FFN_ROW_SPLIT
    for i in range(FFN_ROW_SPLIT):
        rs = slice(i * half, (i + 1) * half)
        o_ref[rs, :] = _ffn_rows(x_ref[rs, :], win_ref, wout_ref, ln_g, ln_b)

    @pl.when(pl.program_id(0) == 0)
    def _():
        odec_ref[...] = _ffn_rows(xdec_ref[...], win_ref, wout_ref, ln_g, ln_b)


def _ffn(x, x_dec, layer, w_in, w_out, ln_g, ln_b):
    n, d = x.shape
    assert n % FFN_ROW_TILE == 0 and D_FF % FF_TILE == 0
    return pl.pallas_call(
        _ffn_kernel,
        grid=(n // FFN_ROW_TILE,),
        in_specs=[
            pl.BlockSpec((FFN_ROW_TILE, d), lambda i: (i, 0)),
            _const_spec(x_dec.shape),
            _const_spec(w_in.shape),
            _const_spec(w_out.shape),
            _layer_spec(ln_g.shape, layer),
            _layer_spec(ln_b.shape, layer),
        ],
        out_specs=[
            pl.BlockSpec((FFN_ROW_TILE, d), lambda i: (i, 0)),
            pl.BlockSpec(x_dec.shape, lambda i: (0, 0)),
        ],
        out_shape=[
            jax.ShapeDtypeStruct((n, d), F32),
            jax.ShapeDtypeStruct(x_dec.shape, F32),
        ],
        compiler_params=_params(1),
        name="ffn",
    )(x, x_dec, w_in, w_out, ln_g, ln_b)


def kernel(x_prompt, x_sample, state_hgrn, ln_mix_g, ln_mix_b, ln_ffn_g, ln_ffn_b, a_lb_raw, a_w_in, a_norm_g, a_w_out, b_w_in, b_ln_g, b_ln_b, b_w_s, b_bias_s, b_w_out, ffn_w_in, ffn_w_out):
    bsz, seq, d = x_prompt.shape
    n_dec = x_sample.shape[0]
    assert x_sample.shape[1] == 1 and seq % CHUNK_B == 0

    masks = jnp.asarray(_level_masks(CHUNK_A))
    signs = jnp.asarray(_level_signs(CHUNK_A, DK_A))
    lb_raw = a_lb_raw.astype(F32)
    bias_full = jnp.repeat(jnp.swapaxes(b_bias_s, 1, 2), DG_B, axis=2)

    def rows(p):
        return p[:, None, :]

    mix_g, mix_b, ffn_g, ffn_b = rows(ln_mix_g), rows(ln_mix_b), rows(ln_ffn_g), rows(ln_ffn_b)
    norm_g, v_g, v_b = rows(a_norm_g), rows(b_ln_g), rows(b_ln_b)

    xp = x_prompt
    xs = x_sample.reshape(n_dec, d)
    st_p, v_p, v_s = [], [], []
    st_s = None
    for layer in range(DEPTH):
        j = layer // N_MIXERS
        if layer % N_MIXERS == 0:
            xp, s, fwi, fwo = _hgrn_prompt(xp, layer, lb_raw, a_w_in, norm_g, a_w_out, mix_g,
                                           mix_b, masks, signs, ffn_w_in, ffn_w_out)
            st_p.append(s)
            xs, st_s = _hgrn_decode(xs, layer, lb_raw, a_w_in, norm_g, a_w_out, mix_g, mix_b,
                                    state_hgrn, st_s)
        else:
            xp, v, xs, vd, fwi, fwo = _cmlp(xp, xs, j, layer, b_w_in, v_g, v_b, b_w_s, bias_full,
                                            b_w_out, mix_g, mix_b, ffn_w_in, ffn_w_out)
            v_p.append(v)
            v_s.append(vd[:, None, :])
        xp, xs = _ffn(xp.reshape(bsz * seq, d), xs, layer, fwi, fwo, ffn_g, ffn_b)
        xp = xp.reshape(bsz, seq, d)

    return (xp, xs.reshape(n_dec, 1, d), jnp.stack(st_p), st_s, jnp.stack(v_p), jnp.stack(v_s))
```

```python
import functools

import numpy as np
import jax
import jax.numpy as jnp
from jax import lax
from jax.experimental import pallas as pl
from jax.experimental.pallas import tpu as pltpu

D_MODEL = 1024
DEPTH = 4
N_MIXERS = 2
DK_A = 128
H_A = D_MODEL // DK_A
DV_A = D_MODEL // H_A
D_INNER_B = D_MODEL
CHUNK_B = 128
G_B = 8
DG_B = D_INNER_B // G_B
D_FF = -(-(8 * D_MODEL) // (3 * 256)) * 256
ALPHA = (2 * DEPTH) ** 0.25
LN_EPS = 1e-5
RMS_EPS = 1e-6

SUBLANES = 8
BF16_SUBLANES = 16
MXU_DIM = 256
VMEM_LIMIT_BYTES = 56 * 2**20

ROW_TILE = 512
FFN_ROW_TILE = 1024
FFN_ROW_SPLIT = 2
CHUNK_A = 128
HEAD_GROUP = 8
STABLE_CHUNK_A = 64
DECAY_LIMIT_LOG2 = 118.0
STABLE_CHUNK_GROUPS = (4, 4)
FF_TILE = MXU_DIM
FFN_CAST_STEPS = 32
DEC_SEQS = 8
DEC_SEQ_GROUP = 8

BF16 = jnp.bfloat16
F32 = jnp.float32


def _level_halves(chunk):
    out, h = [], chunk // 2
    while h >= 1:
        out.append(h)
        h //= 2
    return tuple(out)


LEVEL_HALVES = _level_halves(CHUNK_A)


def _level_masks(chunk):
    t = np.arange(chunk)[:, None]
    s = np.arange(chunk)[None, :]
    masks = [((t // (2 * h) == s // (2 * h)) & (t % (2 * h) >= h) & (s % (2 * h) < h))
             for h in _level_halves(chunk)]
    masks.append(t == s)
    return np.stack(masks).astype(np.float32)


def _level_signs(chunk, width):
    t = np.arange(chunk)[:, None]
    signs = [np.where(t % (2 * h) >= h, 1.0, -1.0) for h in _level_halves(chunk)]
    return np.broadcast_to(np.stack(signs), (len(signs), chunk, width)).astype(np.float32)


def _dot(a, b):
    return lax.dot_general(a, b, (((1,), (0,)), ((), ())), preferred_element_type=F32)


def _dot_nt(a, b):
    return lax.dot_general(a, b, (((1,), (1,)), ((), ())), preferred_element_type=F32)


def _dot_tn(a, b):
    return lax.dot_general(a, b, (((0,), (0,)), ((), ())), preferred_element_type=F32)


def _layer_norm(z, g, b):
    mu = jnp.mean(z, axis=-1, keepdims=True)
    zc = z - mu
    var = jnp.mean(zc * zc, axis=-1, keepdims=True)
    return zc * lax.rsqrt(var + LN_EPS) * g + b


def _gelu(x):
    return 0.5 * x * (1.0 + lax.erf(x * (2.0 ** -0.5)))


def _lower_bound(raw, layer):
    m = jnp.max(raw, axis=0, keepdims=True)
    e = jnp.exp(raw - m)
    p = e / jnp.sum(e, axis=0, keepdims=True)
    c = p[0:1]
    for i in range(1, layer + 1):
        c = c + p[i:i + 1]
    return c - p[0:1]


def _hgrn_gates(xb, win_ref, lb):
    d = D_MODEL
    f = lb + (1.0 - lb) * jax.nn.sigmoid(_dot(xb, win_ref[:, d:2 * d]))
    q = jax.nn.silu(_dot(xb, win_ref[:, 0:d])) * (DK_A ** -0.5)
    gate = jax.nn.silu(_dot(xb, win_ref[:, 3 * d:4 * d]))
    v = _dot(xb, win_ref[:, 2 * d:3 * d])
    return q, f, v, gate


def _split2(a):
    hi = a.astype(BF16)
    lo = (a - hi.astype(F32)).astype(BF16)
    return jnp.concatenate([hi, lo], axis=1)


def _level_ref(g, half):
    n, w = g.shape
    if half >= SUBLANES // 2:
        blk = 2 * half
        parts = [jnp.broadcast_to(g[b * blk + half - 1:b * blk + half, :], (blk, w))
                 for b in range(n // blk)]
        return parts[0] if len(parts) == 1 else jnp.concatenate(parts, axis=0)
    row = lax.broadcasted_iota(jnp.int32, g.shape, 0)
    if half == 2:
        m = row % 4
        return jnp.where(m == 0, pltpu.roll(g, n - 1, axis=0),
                         jnp.where(m == 1, g,
                                   jnp.where(m == 2, pltpu.roll(g, 1, axis=0),
                                             pltpu.roll(g, 2, axis=0))))
    assert half == 1
    return jnp.where(row % 2 == 0, g, pltpu.roll(g, 1, axis=0))


def _rms_norm_rows(o, norm_g):
    ms = jnp.mean(o * o, axis=-1, keepdims=True)
    return o * lax.rsqrt(ms + RMS_EPS) * norm_g


def _hgrn_stable_chunks(heads, chunks, tri, causal, q_s, k_s, v_s, lf_s, oh_s, st_ref, norm_g,
                        between):
    n_rows = STABLE_CHUNK_A
    items = [(c, h) for c in chunks for h in heads]
    n = range(len(items))
    rows = [slice(c * n_rows, (c + 1) * n_rows) for c, _ in items]
    q = [q_s[h, rows[i], :] for i, (_, h) in enumerate(items)]
    k = [k_s[h, rows[i], :] for i, (_, h) in enumerate(items)]
    vb = [v_s[h, rows[i], :] for i, (_, h) in enumerate(items)]
    g2 = [_dot(tri, _split2(lf_s[h, rows[i], :])) for i, (_, h) in enumerate(items)]
    between()
    g = [a[:, :DK_A] + a[:, DK_A:] for a in g2]
    g_last = [a[n_rows - 1:n_rows, :] for a in g]
    qp = [q[i] * jnp.exp2(g[i]).astype(BF16) for i in n]
    ks = [k[i] * jnp.exp2(g_last[i] - g[i]).astype(BF16) for i in n]
    kv = [_dot_tn(vb[i], ks[i]) for i in n]
    a = [causal * _dot_nt(qp[i], k[i] * jnp.exp2(-g[i]).astype(BF16)) for i in n]
    between()
    intra = [_dot(a[i].astype(BF16), vb[i]) for i in n]
    between()
    st = {h: st_ref[h] for h in heads}
    for i, (_, h) in enumerate(items):
        o = intra[i] + _dot_nt(qp[i], st[h].astype(BF16))
        oh_s[h, rows[i], :] = _rms_norm_rows(o, norm_g)
        st[h] = st[h] * jnp.exp2(g_last[i]) + kv[i]
    for h in heads:
        st_ref[h] = st[h]


def _hgrn_general_chunk(heads, c, tri, q_s, k_s, v_s, lf_s, oh_s, st_ref, mask_ref, sign_ref,
                        norm_g):
    rows = pl.ds(pl.multiple_of(c * CHUNK_A, CHUNK_A), CHUNK_A)
    n = range(len(heads))
    q = [q_s[h, rows, :] for h in heads]
    k = [k_s[h, rows, :] for h in heads]
    vb = [v_s[h, rows, :] for h in heads]
    st = [st_ref[h] for h in heads]
    g2 = [_dot(tri, _split2(lf_s[h, rows, :])) for h in heads]
    g = [a[:, :DK_A] + a[:, DK_A:] for a in g2]
    g_last = [a[CHUNK_A - 1:CHUNK_A, :] for a in g]
    o = [_dot_nt(q[i] * jnp.exp2(g[i]).astype(BF16), st[i].astype(BF16)) for i in n]
    ks = [k[i] * jnp.exp2(g_last[i] - g[i]).astype(BF16) for i in n]
    st_new = [st[i] * jnp.exp2(g_last[i]) + _dot_tn(vb[i], ks[i]) for i in n]
    diag = mask_ref[len(LEVEL_HALVES)]
    a = [diag * _dot_nt(q[i], k[i]) for i in n]
    for lvl, half in enumerate(LEVEL_HALVES):
        e = [jnp.exp2((g[i] - _level_ref(g[i], half)) * sign_ref[lvl]).astype(BF16) for i in n]
        p = [_dot_nt(q[i] * e[i], k[i] * e[i]) for i in n]
        a = [a[i] + mask_ref[lvl] * p[i] for i in n]
    o = [o[i] + _dot(a[i].astype(BF16), vb[i]) for i in n]
    for i, h in enumerate(heads):
        st_ref[h] = st_new[i]
        oh_s[h, rows, :] = _rms_norm_rows(o[i], norm_g)


PROJ_TILES = [(section, tile) for section in (1, 0, 3, 2) for tile in range(D_MODEL // MXU_DIM)]
HEADS_PER_TILE = MXU_DIM // DK_A
PROJ_TILES_PER_SLOT = (4, 2, 1, 1)


def _hgrn_project_tile(section, tile, xb, win_ref, lb):
    c0 = section * D_MODEL + tile * MXU_DIM
    a = _dot(xb, win_ref[:, c0:c0 + MXU_DIM])
    if section == 0:
        return (jax.nn.silu(a) * (DK_A ** -0.5)).astype(BF16)
    if section == 1:
        lbt = lb[:, tile * MXU_DIM:(tile + 1) * MXU_DIM]
        f = lbt + (1.0 - lbt) * jax.nn.sigmoid(a)
        lf = jnp.log2(f)
        n_chunks = lf.shape[0] // STABLE_CHUNK_A
        decay = jnp.min(jnp.sum(lf.reshape(n_chunks, STABLE_CHUNK_A, MXU_DIM), axis=1))
        return (1.0 - f).astype(BF16), lf, decay
    if section == 2:
        return a.astype(BF16)
    return jax.nn.silu(a)


def _hgrn_store_projection(pieces, q_s, k_s, v_s, lf_s, gate_s, stable_s):
    decay = None
    for tile in range(D_MODEL // MXU_DIM):
        k, lf, tile_decay = pieces[(1, tile)]
        decay = tile_decay if decay is None else jnp.minimum(decay, tile_decay)
        gate_s[:, tile * MXU_DIM:(tile + 1) * MXU_DIM] = pieces[(3, tile)]
        for i in range(HEADS_PER_TILE):
            h = tile * HEADS_PER_TILE + i
            ls = slice(i * DK_A, (i + 1) * DK_A)
            q_s[h] = pieces[(0, tile)][:, ls]
            k_s[h] = k[:, ls]
            v_s[h] = pieces[(2, tile)][:, ls]
            lf_s[h] = lf[:, ls]
    stable_s[0] = (decay >= -DECAY_LIMIT_LOG2).astype(jnp.int32)


def _hgrn_prompt_kernel(layer, x_ref, xnext_ref, lbraw_ref, win_ref, ng_ref, wout_ref, lng_ref,
                        lnb_ref, mask_ref, sign_ref, fwi_ref, fwo_ref, o_ref, sfin_ref, fwib_ref,
                        fwob_ref, st_ref, q_s, k_s, v_s, lf_s, oh_s, gate_s, stable_s):
    t = pl.program_id(1)
    _cast_ffn_slabs(fwi_ref, fwo_ref, fwib_ref, fwob_ref)
    lb = _lower_bound(lbraw_ref[...], layer)

    def project(x):
        xb = x.astype(BF16)
        return {st: _hgrn_project_tile(st[0], st[1], xb, win_ref, lb) for st in PROJ_TILES}

    def store(pieces):
        _hgrn_store_projection(pieces, q_s, k_s, v_s, lf_s, gate_s, stable_s)

    @pl.when((pl.program_id(0) == 0) & (t == 0))
    def _():
        store(project(x_ref[0]))

    @pl.when(t == 0)
    def _():
        st_ref[...] = jnp.zeros_like(st_ref)

    norm_g = ng_ref[...]

    def lower_triangle(n_rows):
        row = lax.broadcasted_iota(jnp.int32, (n_rows, n_rows), 0)
        col = lax.broadcasted_iota(jnp.int32, (n_rows, n_rows), 1)
        return row >= col

    def finish_rows(rows):
        o = jnp.concatenate([oh_s[h, rows, :] for h in range(H_A)], axis=1) * gate_s[rows, :]
        y = _dot(o.astype(BF16), wout_ref[...])
        o_ref[0, rows, :] = _layer_norm(ALPHA * x_ref[0, rows, :] + y, lng_ref[...], lnb_ref[...])

    def run_stable():
        causal = lower_triangle(STABLE_CHUNK_A)
        tri = causal.astype(BF16)
        causal = causal.astype(F32)
        heads = list(range(H_A))
        xb = xnext_ref[0].astype(BF16)
        todo = list(PROJ_TILES)
        pieces = {}
        share = iter(PROJ_TILES_PER_SLOT * len(STABLE_CHUNK_GROUPS))

        def between():
            for st in todo[:next(share)]:
                pieces[st] = _hgrn_project_tile(st[0], st[1], xb, win_ref, lb)
                todo.remove(st)

        c0 = 0
        for group in STABLE_CHUNK_GROUPS:
            _hgrn_stable_chunks(heads, list(range(c0, c0 + group)), tri, causal, q_s, k_s, v_s,
                                lf_s, oh_s, st_ref, norm_g, between)
            finish_rows(slice(c0 * STABLE_CHUNK_A, (c0 + group) * STABLE_CHUNK_A))
            between()
            c0 += group
        assert not todo
        store(pieces)

    def run_general():
        tri = lower_triangle(CHUNK_A).astype(BF16)

        def group_body(hg, carry):
            heads = [hg * HEAD_GROUP + i for i in range(HEAD_GROUP)]

            def chunk_body(c, carry2):
                _hgrn_general_chunk(heads, c, tri, q_s, k_s, v_s, lf_s, oh_s, st_ref, mask_ref,
                                    sign_ref, norm_g)
                return carry2
            return lax.fori_loop(0, ROW_TILE // CHUNK_A, chunk_body, carry)

        lax.fori_loop(0, H_A // HEAD_GROUP, group_body, 0)
        finish_rows(slice(0, ROW_TILE))
        store(project(xnext_ref[0]))

    lax.cond(stable_s[0] != 0, run_stable, run_general)

    @pl.when(t == pl.num_programs(1) - 1)
    def _():
        for h in range(H_A):
            sfin_ref[0, h] = st_ref[h].T


def _const_spec(shape):
    nd = len(shape)
    return pl.BlockSpec(shape, lambda *_: (0,) * nd, pipeline_mode=pl.Buffered(1))


def _layer_spec(shape, j):
    nd = len(shape)
    return pl.BlockSpec((None,) + tuple(shape[1:]), lambda *_: (j,) + (0,) * (nd - 1),
                        pipeline_mode=pl.Buffered(1))


def _ffn_cast_specs(ffn_w_in, ffn_w_out, layer, steps_per_row):
    _, d, ff2 = ffn_w_in.shape
    _, ff, _ = ffn_w_out.shape
    steps = FFN_CAST_STEPS
    rows_in, rows_out = d // steps, 2 * ff // steps
    assert d % steps == 0 and (2 * ff) % steps == 0
    assert rows_in % BF16_SUBLANES == 0 and rows_out % BF16_SUBLANES == 0

    def step(i, t):
        return i * steps_per_row + t

    in_specs = [
        pl.BlockSpec((None, rows_in, ff2), lambda i, t: (layer, step(i, t), 0)),
        pl.BlockSpec((None, rows_out, d), lambda i, t: (layer, step(i, t) // 2, 0)),
    ]
    out_specs = [
        pl.BlockSpec((rows_in, ff2), lambda i, t: (step(i, t), 0)),
        pl.BlockSpec((rows_out, d), lambda i, t: (step(i, t) // 2, 0)),
    ]
    out_shapes = [jax.ShapeDtypeStruct((d, ff2), BF16), jax.ShapeDtypeStruct((ff, d), BF16)]
    return in_specs, out_specs, out_shapes


def _cast_ffn_slabs(fwi_ref, fwo_ref, fwib_ref, fwob_ref):
    fwib_ref[...] = fwi_ref[...].astype(BF16)
    fwob_ref[...] = fwo_ref[...].astype(BF16)


def _params(n_axes):
    return pltpu.CompilerParams(dimension_semantics=("arbitrary",) * n_axes,
                                vmem_limit_bytes=VMEM_LIMIT_BYTES)


def _hgrn_prompt(x, layer, lb_raw, w_in, norm_g, w_out, ln_g, ln_b, masks, signs, ffn_w_in,
                 ffn_w_out):
    b, l, d = x.shape
    j = layer // N_MIXERS
    assert b * (l // ROW_TILE) == FFN_CAST_STEPS
    cast_in, cast_out, cast_shapes = _ffn_cast_specs(ffn_w_in, ffn_w_out, layer, l // ROW_TILE)
    assert l % ROW_TILE == 0 and ROW_TILE % CHUNK_A == 0 and H_A % HEAD_GROUP == 0
    assert sum(STABLE_CHUNK_GROUPS) * STABLE_CHUNK_A == ROW_TILE
    nt = l // ROW_TILE

    def next_tile(i, t):
        last = (i == b - 1) & (t == nt - 1)
        wrap = t == nt - 1
        return (jnp.where(wrap & ~last, i + 1, i), jnp.where(last, t, jnp.where(wrap, 0, t + 1)), 0)

    head_operand = pltpu.VMEM((H_A, ROW_TILE, DK_A), BF16)
    return pl.pallas_call(
        functools.partial(_hgrn_prompt_kernel, layer),
        grid=(b, l // ROW_TILE),
        in_specs=[
            pl.BlockSpec((1, ROW_TILE, d), lambda i, t: (i, t, 0)),
            pl.BlockSpec((1, ROW_TILE, d), next_tile),
            _const_spec(lb_raw.shape),
            _layer_spec(w_in.shape, j),
            _layer_spec(norm_g.shape, j),
            _layer_spec(w_out.shape, j),
            _layer_spec(ln_g.shape, layer),
            _layer_spec(ln_b.shape, layer),
            _const_spec(masks.shape),
            _const_spec(signs.shape),
        ] + cast_in,
        out_specs=[
            pl.BlockSpec((1, ROW_TILE, d), lambda i, t: (i, t, 0)),
            pl.BlockSpec((1, H_A, DK_A, DV_A), lambda i, t: (i, 0, 0, 0)),
        ] + cast_out,
        out_shape=[
            jax.ShapeDtypeStruct((b, l, d), F32),
            jax.ShapeDtypeStruct((b, H_A, DK_A, DV_A), F32),
        ] + cast_shapes,
        scratch_shapes=[
            pltpu.VMEM((H_A, DV_A, DK_A), F32),
            head_operand, head_operand, head_operand,
            pltpu.VMEM((H_A, ROW_TILE, DK_A), F32),
            pltpu.VMEM((H_A, ROW_TILE, DV_A), F32),
            pltpu.VMEM((ROW_TILE, d), F32),
            pltpu.SMEM((1,), jnp.int32),
        ],
        compiler_params=_params(2),
        name="hgrn_prompt",
    )(x, x, lb_raw, w_in, norm_g, w_out, ln_g, ln_b, masks, signs, ffn_w_in, ffn_w_out)


def _split3(f):
    hi = f.astype(BF16).astype(F32)
    r = f - hi
    mid = r.astype(BF16).astype(F32)
    lo = (r - mid).astype(BF16).astype(F32)
    return hi, mid, lo


def _hgrn_decode_kernel(layer, chained, x_ref, lbraw_ref, win_ref, ng_ref, wout_ref, lng_ref,
                        lnb_ref, s_ref, *rest):
    o_ref, snew_ref, f3_s, k_s, q_s, v_s, gate_s, oacc_s = rest[1:] if chained else rest

    @pl.when(pl.program_id(0) == 0)
    def _():
        _hgrn_decode_step(layer, x_ref, lbraw_ref, win_ref, ng_ref, wout_ref, lng_ref, lnb_ref,
                          s_ref, o_ref, snew_ref, f3_s, k_s, q_s, v_s, gate_s, oacc_s)

    @pl.when(pl.program_id(0) > 0)
    def _():
        snew_ref[...] = jnp.zeros_like(snew_ref)


def _hgrn_decode_step(layer, x_ref, lbraw_ref, win_ref, ng_ref, wout_ref, lng_ref, lnb_ref,
                      s_ref, o_ref, snew_ref, f3_s, k_s, q_s, v_s, gate_s, oacc_s):
    step = pl.program_id(1)
    d = D_MODEL
    nb = DEC_SEQS

    @pl.when(step == 0)
    def _():
        q, f, v, gate = _hgrn_gates(x_ref[...].astype(BF16), win_ref,
                                    _lower_bound(lbraw_ref[...], layer))
        hi, mid, lo = _split3(f)
        f3_s[0] = hi
        f3_s[1] = mid
        f3_s[2] = lo
        k_s[...] = 1.0 - f
        q_s[...] = q
        v_s[...] = v
        gate_s[...] = gate

    rows = pl.ds(pl.multiple_of(step * nb, nb), nb)
    fhi = f3_s[0, rows, :]
    fmid = f3_s[1, rows, :]
    flo = f3_s[2, rows, :]
    k8 = k_s[rows, :]
    q8 = q_s[rows, :]
    v8 = v_s[rows, :]
    ones = jnp.ones((nb, DV_A), F32)
    zeros = jnp.zeros((nb, DV_A), F32)
    rhs = []
    for h in range(H_A):
        ls = slice(h * DK_A, (h + 1) * DK_A)
        left = jnp.concatenate([ones, ones, ones, zeros], axis=0)
        right = jnp.concatenate([zeros, zeros, zeros, v8[:, ls]], axis=0)
        rhs.append(jnp.concatenate([left, right], axis=1).astype(BF16))
    seq = lax.broadcasted_iota(jnp.int32, (nb, DK_A), 0)
    head_lanes = [slice(h * DK_A, (h + 1) * DK_A) for h in range(H_A)]

    def group_body(p, o8):
        seqs = [p * DEC_SEQ_GROUP + i for i in range(DEC_SEQ_GROUP)]
        sels = [seq == i for i in seqs]
        chains = [(si, h) for si in range(DEC_SEQ_GROUP) for h in range(H_A)]
        s_old = [s_ref[seqs[si], h] for si, h in chains]
        lhs = [jnp.concatenate([jnp.where(sels[si], a[:, head_lanes[h]], 0.0)
                                for a in (fhi, fmid, flo, k8)], axis=0).astype(BF16)
               for si, h in chains]
        fb_kv = [_dot_tn(lhs[c], rhs[h]) for c, (si, h) in enumerate(chains)]
        s_new = [fb_kv[c][:, :DV_A] * s_old[c] + fb_kv[c][:, DV_A:] for c in range(len(chains))]
        qsel = [jnp.where(sels[si], q8[:, head_lanes[h]], 0.0).astype(BF16) for si, h in chains]
        outs = [_dot(qsel[c], s_new[c].astype(BF16)) for c in range(len(chains))]
        for c, (si, h) in enumerate(chains):
            snew_ref[seqs[si], h] = s_new[c]
        for si in range(DEC_SEQ_GROUP):
            o8 = o8 + jnp.concatenate(outs[si * H_A:(si + 1) * H_A], axis=1)
        return o8

    oacc_s[rows, :] = lax.fori_loop(0, nb // DEC_SEQ_GROUP, group_body, jnp.zeros((nb, d), F32))

    @pl.when(step == pl.num_programs(1) - 1)
    def _():
        o = oacc_s[...]
        norm_g = ng_ref[...]
        heads = []
        for h in range(H_A):
            oh = o[:, h * DV_A:(h + 1) * DV_A]
            ms = jnp.mean(oh * oh, axis=-1, keepdims=True)
            heads.append(oh * lax.rsqrt(ms + RMS_EPS) * norm_g)
        on = jnp.concatenate(heads, axis=1) * gate_s[...]
        y = _dot(on.astype(BF16), wout_ref[...])
        o_ref[...] = _layer_norm(ALPHA * x_ref[...] + y, lng_ref[...], lnb_ref[...])


def _hgrn_decode(x, layer, lb_raw, w_in, norm_g, w_out, ln_g, ln_b, states, new_states):
    n, d = x.shape
    j = layer // N_MIXERS
    assert n % DEC_SEQS == 0 and DEC_SEQS % DEC_SEQ_GROUP == 0
    chained = new_states is not None
    steps = n // DEC_SEQS
    passes = 1 if chained else states.shape[0] - j
    row_scratch = pltpu.VMEM((n, d), F32)
    state_block = (None, DEC_SEQS, H_A, DK_A, DV_A)
    state_in = pl.BlockSpec(state_block,
                            lambda g, s: (j, jnp.where(g == 0, s, steps - 1), 0, 0, 0))
    state_out = pl.BlockSpec(state_block, lambda g, s: (j + g, s, 0, 0, 0))
    operands = [x, lb_raw, w_in, norm_g, w_out, ln_g, ln_b, states]
    in_specs = [
        _const_spec(x.shape),
        _const_spec(lb_raw.shape),
        _layer_spec(w_in.shape, j),
        _layer_spec(norm_g.shape, j),
        _layer_spec(w_out.shape, j),
        _layer_spec(ln_g.shape, layer),
        _layer_spec(ln_b.shape, layer),
        state_in,
    ]
    if chained:
        operands.append(new_states)
        in_specs.append(pl.BlockSpec(memory_space=pl.ANY))
    return pl.pallas_call(
        functools.partial(_hgrn_decode_kernel, layer, chained),
        grid=(passes, steps),
        in_specs=in_specs,
        out_specs=[pl.BlockSpec((n, d), lambda g, s: (0, 0)), state_out],
        out_shape=[
            jax.ShapeDtypeStruct((n, d), F32),
            jax.ShapeDtypeStruct(states.shape, F32),
        ],
        scratch_shapes=[
            pltpu.VMEM((3, n, d), F32),
            row_scratch, row_scratch, row_scratch, row_scratch, row_scratch,
        ],
        input_output_aliases={len(operands) - 1: 1} if chained else {},
        compiler_params=_params(2),
        name="hgrn_decode",
    )(*operands)


def _cmlp_gate_values(x, win_ref, vg_ref, vb_ref):
    xb = x.astype(BF16)
    di = D_INNER_B
    v = _layer_norm(_gelu(_dot(xb, win_ref[:, di:2 * di])), vg_ref[...], vb_ref[...])
    u = _gelu(_dot(xb, win_ref[:, 0:di]))
    return u, v


def _cmlp_finish(x, u, mixed, wout_ref, lng_ref, lnb_ref):
    y = _dot((u * mixed).astype(BF16), wout_ref[...])
    return _layer_norm(ALPHA * x + y, lng_ref[...], lnb_ref[...])


def _cmlp_kernel(x_ref, xdec_ref, win_ref, vg_ref, vb_ref, ws_ref, bias_ref, wout_ref,
                 lng_ref, lnb_ref, fwi_ref, fwo_ref, o_ref, vrows_ref, odec_ref, vdec_ref,
                 fwib_ref, fwob_ref):
    _cast_ffn_slabs(fwi_ref, fwo_ref, fwib_ref, fwob_ref)
    x = x_ref[...]
    n = x.shape[0]
    u, v = _cmlp_gate_values(x, win_ref, vg_ref, vb_ref)
    row = lax.broadcasted_iota(jnp.int32, (CHUNK_B, CHUNK_B), 0)
    col = lax.broadcasted_iota(jnp.int32, (CHUNK_B, CHUNK_B), 1)
    w = [jnp.where(row >= col, ws_ref[g], 0.0).astype(BF16) for g in range(G_B)]
    bias = bias_ref[...]
    vb16 = v.astype(BF16)
    chunks = []
    for c in range(n // CHUNK_B):
        rs = slice(c * CHUNK_B, (c + 1) * CHUNK_B)
        groups = [_dot(w[g], vb16[rs, g * DG_B:(g + 1) * DG_B]) for g in range(G_B)]
        chunks.append(jnp.concatenate(groups, axis=1) + bias)
    o_ref[...] = _cmlp_finish(x, u, jnp.concatenate(chunks, axis=0), wout_ref, lng_ref, lnb_ref)

    @pl.when(pl.program_id(1) == pl.num_programs(1) - 1)
    def _():
        vrows_ref[...] = v[n - CHUNK_B:, :]

    @pl.when((pl.program_id(0) == 0) & (pl.program_id(1) == 0))
    def _():
        xd = xdec_ref[...]
        ud, vd = _cmlp_gate_values(xd, win_ref, vg_ref, vb_ref)
        w00 = jnp.concatenate([jnp.broadcast_to(ws_ref[g, 0:1, 0:1], (1, DG_B))
                               for g in range(G_B)], axis=1)
        vdec_ref[...] = vd
        odec_ref[...] = _cmlp_finish(xd, ud, vd * w00 + bias[0:1, :], wout_ref, lng_ref, lnb_ref)


def _cmlp(x, x_dec, j, layer, w_in, v_g, v_b, w_s, bias_full, w_out, ln_g, ln_b, ffn_w_in,
          ffn_w_out):
    b, l, d = x.shape
    assert l % ROW_TILE == 0 and ROW_TILE % CHUNK_B == 0
    assert b * (l // ROW_TILE) == FFN_CAST_STEPS
    cast_in, cast_out, cast_shapes = _ffn_cast_specs(ffn_w_in, ffn_w_out, layer, l // ROW_TILE)
    dec_out = pl.BlockSpec(x_dec.shape, lambda i, t: (0, 0))
    return pl.pallas_call(
        _cmlp_kernel,
        grid=(b, l // ROW_TILE),
        in_specs=[
            pl.BlockSpec((None, ROW_TILE, d), lambda i, t: (i, t, 0)),
            _const_spec(x_dec.shape),
            _layer_spec(w_in.shape, j),
            _layer_spec(v_g.shape, j),
            _layer_spec(v_b.shape, j),
            _layer_spec(w_s.shape, j),
            _layer_spec(bias_full.shape, j),
            _layer_spec(w_out.shape, j),
            _layer_spec(ln_g.shape, layer),
            _layer_spec(ln_b.shape, layer),
        ] + cast_in,
        out_specs=[
            pl.BlockSpec((None, ROW_TILE, d), lambda i, t: (i, t, 0)),
            pl.BlockSpec((None, CHUNK_B, D_INNER_B), lambda i, t: (i, 0, 0)),
            dec_out,
            dec_out,
        ] + cast_out,
        out_shape=[
            jax.ShapeDtypeStruct((b, l, d), F32),
            jax.ShapeDtypeStruct((b, CHUNK_B, D_INNER_B), F32),
            jax.ShapeDtypeStruct(x_dec.shape, F32),
            jax.ShapeDtypeStruct(x_dec.shape, F32),
        ] + cast_shapes,
        compiler_params=_params(2),
        name="cmlp",
    )(x, x_dec, w_in, v_g, v_b, w_s, bias_full, w_out, ln_g, ln_b, ffn_w_in, ffn_w_out)


def _ffn_rows(x, win_ref, wout_ref, ln_g, ln_b):
    xb = x.astype(BF16)
    acc = jnp.zeros(x.shape, F32)
    for j in range(D_FF // FF_TILE):
        cs = slice(j * FF_TILE, (j + 1) * FF_TILE)
        us = slice(D_FF + j * FF_TILE, D_FF + (j + 1) * FF_TILE)
        hcol = jax.nn.silu(_dot(xb, win_ref[:, cs])) * _dot(xb, win_ref[:, us])
        acc = acc + _dot(hcol.astype(BF16), wout_ref[cs, :])
    return _layer_norm(ALPHA * x + acc, ln_g, ln_b)


def _ffn_kernel(x_ref, xdec_ref, win_ref, wout_ref, lng_ref, lnb_ref, o_ref, odec_ref):
    ln_g, ln_b = lng_ref[...], lnb_ref[...]
    half = x_ref.shape[0] // FFN_ROW_SPLIT
    for i in range(FFN_ROW_SPLIT):
        rs = slice(i * half, (i + 1) * half)
        o_ref[rs, :] = _ffn_rows(x_ref[rs, :], win_ref, wout_ref, ln_g, ln_b)

    @pl.when(pl.program_id(0) == 0)
    def _():
        odec_ref[...] = _ffn_rows(xdec_ref[...], win_ref, wout_ref, ln_g, ln_b)


def _ffn(x, x_dec, layer, w_in, w_out, ln_g, ln_b):
    n, d = x.shape
    assert n % FFN_ROW_TILE == 0 and D_FF % FF_TILE == 0
    return pl.pallas_call(
        _ffn_kernel,
        grid=(n // FFN_ROW_TILE,),
        in_specs=[
            pl.BlockSpec((FFN_ROW_TILE, d), lambda i: (i, 0)),
            _const_spec(x_dec.shape),
            _const_spec(w_in.shape),
            _const_spec(w_out.shape),
            _layer_spec(ln_g.shape, layer),
            _layer_spec(ln_b.shape, layer),
        ],
        out_specs=[
            pl.BlockSpec((FFN_ROW_TILE, d), lambda i: (i, 0)),
            pl.BlockSpec(x_dec.shape, lambda i: (0, 0)),
        ],
        out_shape=[
            jax.ShapeDtypeStruct((n, d), F32),
            jax.ShapeDtypeStruct(x_dec.shape, F32),
        ],
        compiler_params=_params(1),
        name="ffn",
    )(x, x_dec, w_in, w_out, ln_g, ln_b)


def kernel(x_prompt, x_sample, state_hgrn, ln_mix_g, ln_mix_b, ln_ffn_g, ln_ffn_b, a_lb_raw, a_w_in, a_norm_g, a_w_out, b_w_in, b_ln_g, b_ln_b, b_w_s, b_bias_s, b_w_out, ffn_w_in, ffn_w_out):
    bsz, seq, d = x_prompt.shape
    n_dec = x_sample.shape[0]
    assert x_sample.shape[1] == 1 and seq % CHUNK_B == 0

    masks = jnp.asarray(_level_masks(CHUNK_A))
    signs = jnp.asarray(_level_signs(CHUNK_A, DK_A))
    lb_raw = a_lb_raw.astype(F32)
    bias_full = jnp.repeat(jnp.swapaxes(b_bias_s, 1, 2), DG_B, axis=2)

    def rows(p):
        return p[:, None, :]

    mix_g, mix_b, ffn_g, ffn_b = rows(ln_mix_g), rows(ln_mix_b), rows(ln_ffn_g), rows(ln_ffn_b)
    norm_g, v_g, v_b = rows(a_norm_g), rows(b_ln_g), rows(b_ln_b)

    xp = x_prompt
    xs = x_sample.reshape(n_dec, d)
    st_p, v_p, v_s = [], [], []
    st_s = None
    for layer in range(DEPTH):
        j = layer // N_MIXERS
        if layer % N_MIXERS == 0:
            xp, s, fwi, fwo = _hgrn_prompt(xp, layer, lb_raw, a_w_in, norm_g, a_w_out, mix_g,
                                           mix_b, masks, signs, ffn_w_in, ffn_w_out)
            st_p.append(s)
            xs, st_s = _hgrn_decode(xs, layer, lb_raw, a_w_in, norm_g, a_w_out, mix_g, mix_b,
                                    state_hgrn, st_s)
        else:
            xp, v, xs, vd, fwi, fwo = _cmlp(xp, xs, j, layer, b_w_in, v_g, v_b, b_w_s, bias_full,
                                            b_w_out, mix_g, mix_b, ffn_w_in, ffn_w_out)
            v_p.append(v)
            v_s.append(vd[:, None, :])
        xp, xs = _ffn(xp.reshape(bsz * seq, d), xs, layer, fwi, fwo, ffn_g, ffn_b)
        xp = xp.reshape(bsz, seq, d)

    return (xp, xs.reshape(n_dec, 1, d), jnp.stack(st_p), st_s, jnp.stack(v_p), jnp.stack(v_s))
```

```python
import functools

import numpy as np
import jax
import jax.numpy as jnp
from jax import lax
from jax.experimental import pallas as pl
from jax.experimental.pallas import tpu as pltpu

D_MODEL = 1024
DEPTH = 4
N_MIXERS = 2
DK_A = 128
H_A = D_MODEL // DK_A
DV_A = D_MODEL // H_A
D_INNER_B = D_MODEL
CHUNK_B = 128
G_B = 8
DG_B = D_INNER_B // G_B
D_FF = -(-(8 * D_MODEL) // (3 * 256)) * 256
ALPHA = (2 * DEPTH) ** 0.25
LN_EPS = 1e-5
RMS_EPS = 1e-6

SUBLANES = 8
BF16_SUBLANES = 16
MXU_DIM = 256
VMEM_LIMIT_BYTES = 56 * 2**20

ROW_TILE = 512
FFN_ROW_TILE = 1024
FFN_ROW_SPLIT = 2
CHUNK_A = 128
HEAD_GROUP = 8
STABLE_CHUNK_A = 64
DECAY_LIMIT_LOG2 = 118.0
STABLE_CHUNK_GROUPS = (4, 4)
FF_TILE = MXU_DIM
FFN_CAST_STEPS = 32
DEC_SEQS = 8
DEC_SEQ_GROUP = 8

BF16 = jnp.bfloat16
F32 = jnp.float32


def _level_halves(chunk):
    out, h = [], chunk // 2
    while h >= 1:
        out.append(h)
        h //= 2
    return tuple(out)


LEVEL_HALVES = _level_halves(CHUNK_A)


def _level_masks(chunk):
    t = np.arange(chunk)[:, None]
    s = np.arange(chunk)[None, :]
    masks = [((t // (2 * h) == s // (2 * h)) & (t % (2 * h) >= h) & (s % (2 * h) < h))
             for h in _level_halves(chunk)]
    masks.append(t == s)
    return np.stack(masks).astype(np.float32)


def _level_signs(chunk, width):
    t = np.arange(chunk)[:, None]
    signs = [np.where(t % (2 * h) >= h, 1.0, -1.0) for h in _level_halves(chunk)]
    return np.broadcast_to(np.stack(signs), (len(signs), chunk, width)).astype(np.float32)


def _dot(a, b):
    return lax.dot_general(a, b, (((1,), (0,)), ((), ())), preferred_element_type=F32)


def _dot_nt(a, b):
    return lax.dot_general(a, b, (((1,), (1,)), ((), ())), preferred_element_type=F32)


def _dot_tn(a, b):
    return lax.dot_general(a, b, (((0,), (0,)), ((), ())), preferred_element_type=F32)


def _layer_norm(z, g, b):
    mu = jnp.mean(z, axis=-1, keepdims=True)
    zc = z - mu
    var = jnp.mean(zc * zc, axis=-1, keepdims=True)
    return zc * lax.rsqrt(var + LN_EPS) * g + b


def _gelu(x):
    return 0.5 * x * (1.0 + lax.erf(x * (2.0 ** -0.5)))


def _lower_bound(raw, layer):
    m = jnp.max(raw, axis=0, keepdims=True)
    e = jnp.exp(raw - m)
    p = e / jnp.sum(e, axis=0, keepdims=True)
    c = p[0:1]
    for i in range(1, layer + 1):
        c = c + p[i:i + 1]
    return c - p[0:1]


def _hgrn_gates(xb, win_ref, lb):
    d = D_MODEL
    f = lb + (1.0 - lb) * jax.nn.sigmoid(_dot(xb, win_ref[:, d:2 * d]))
    q = jax.nn.silu(_dot(xb, win_ref[:, 0:d])) * (DK_A ** -0.5)
    gate = jax.nn.silu(_dot(xb, win_ref[:, 3 * d:4 * d]))
    v = _dot(xb, win_ref[:, 2 * d:3 * d])
    return q, f, v, gate


def _split2(a):
    hi = a.astype(BF16)
    lo = (a - hi.astype(F32)).astype(BF16)
    return jnp.concatenate([hi, lo], axis=1)


def _level_ref(g, half):
    n, w = g.shape
    if half >= SUBLANES // 2:
        blk = 2 * half
        parts = [jnp.broadcast_to(g[b * blk + half - 1:b * blk + half, :], (blk, w))
                 for b in range(n // blk)]
        return parts[0] if len(parts) == 1 else jnp.concatenate(parts, axis=0)
    row = lax.broadcasted_iota(jnp.int32, g.shape, 0)
    if half == 2:
        m = row % 4
        return jnp.where(m == 0, pltpu.roll(g, n - 1, axis=0),
                         jnp.where(m == 1, g,
                                   jnp.where(m == 2, pltpu.roll(g, 1, axis=0),
                                             pltpu.roll(g, 2, axis=0))))
    assert half == 1
    return jnp.where(row % 2 == 0, g, pltpu.roll(g, 1, axis=0))


def _rms_norm_rows(o, norm_g):
    ms = jnp.mean(o * o, axis=-1, keepdims=True)
    return o * lax.rsqrt(ms + RMS_EPS) * norm_g


def _hgrn_stable_chunks(heads, chunks, tri, causal, q_s, k_s, v_s, lf_s, oh_s, st_ref, norm_g,
                        between):
    n_rows = STABLE_CHUNK_A
    items = [(c, h) for c in chunks for h in heads]
    n = range(len(items))
    rows = [slice(c * n_rows, (c + 1) * n_rows) for c, _ in items]
    q = [q_s[h, rows[i], :] for i, (_, h) in enumerate(items)]
    k = [k_s[h, rows[i], :] for i, (_, h) in enumerate(items)]
    vb = [v_s[h, rows[i], :] for i, (_, h) in enumerate(items)]
    g2 = [_dot(tri, _split2(lf_s[h, rows[i], :])) for i, (_, h) in enumerate(items)]
    between()
    g = [a[:, :DK_A] + a[:, DK_A:] for a in g2]
    g_last = [a[n_rows - 1:n_rows, :] for a in g]
    qp = [q[i] * jnp.exp2(g[i]).astype(BF16) for i in n]
    ks = [k[i] * jnp.exp2(g_last[i] - g[i]).astype(BF16) for i in n]
    kv = [_dot_tn(vb[i], ks[i]) for i in n]
    a = [causal * _dot_nt(qp[i], k[i] * jnp.exp2(-g[i]).astype(BF16)) for i in n]
    between()
    intra = [_dot(a[i].astype(BF16), vb[i]) for i in n]
    between()
    st = {h: st_ref[h] for h in heads}
    for i, (_, h) in enumerate(items):
        o = intra[i] + _dot_nt(qp[i], st[h].astype(BF16))
        oh_s[h, rows[i], :] = _rms_norm_rows(o, norm_g)
        st[h] = st[h] * jnp.exp2(g_last[i]) + kv[i]
    for h in heads:
        st_ref[h] = st[h]


def _hgrn_general_chunk(heads, c, tri, q_s, k_s, v_s, lf_s, oh_s, st_ref, mask_ref, sign_ref,
                        norm_g):
    rows = pl.ds(pl.multiple_of(c * CHUNK_A, CHUNK_A), CHUNK_A)
    n = range(len(heads))
    q = [q_s[h, rows, :] for h in heads]
    k = [k_s[h, rows, :] for h in heads]
    vb = [v_s[h, rows, :] for h in heads]
    st = [st_ref[h] for h in heads]
    g2 = [_dot(tri, _split2(lf_s[h, rows, :])) for h in heads]
    g = [a[:, :DK_A] + a[:, DK_A:] for a in g2]
    g_last = [a[CHUNK_A - 1:CHUNK_A, :] for a in g]
    o = [_dot_nt(q[i] * jnp.exp2(g[i]).astype(BF16), st[i].astype(BF16)) for i in n]
    ks = [k[i] * jnp.exp2(g_last[i] - g[i]).astype(BF16) for i in n]
    st_new = [st[i] * jnp.exp2(g_last[i]) + _dot_tn(vb[i], ks[i]) for i in n]
    diag = mask_ref[len(LEVEL_HALVES)]
    a = [diag * _dot_nt(q[i], k[i]) for i in n]
    for lvl, half in enumerate(LEVEL_HALVES):
        e = [jnp.exp2((g[i] - _level_ref(g[i], half)) * sign_ref[lvl]).astype(BF16) for i in n]
        p = [_dot_nt(q[i] * e[i], k[i] * e[i]) for i in n]
        a = [a[i] + mask_ref[lvl] * p[i] for i in n]
    o = [o[i] + _dot(a[i].astype(BF16), vb[i]) for i in n]
    for i, h in enumerate(heads):
        st_ref[h] = st_new[i]
        oh_s[h, rows, :] = _rms_norm_rows(o[i], norm_g)


PROJ_TILES = [(section, tile) for section in (1, 0, 3, 2) for tile in range(D_MODEL // MXU_DIM)]
HEADS_PER_TILE = MXU_DIM // DK_A
PROJ_TILES_PER_SLOT = (3, 2, 1, 2)


def _hgrn_project_tile(section, tile, xb, win_ref, lb):
    c0 = section * D_MODEL + tile * MXU_DIM
    a = _dot(xb, win_ref[:, c0:c0 + MXU_DIM])
    if section == 0:
        return (jax.nn.silu(a) * (DK_A ** -0.5)).astype(BF16)
    if section == 1:
        lbt = lb[:, tile * MXU_DIM:(tile + 1) * MXU_DIM]
        f = lbt + (1.0 - lbt) * jax.nn.sigmoid(a)
        lf = jnp.log2(f)
        n_chunks = lf.shape[0] // STABLE_CHUNK_A
        decay = jnp.min(jnp.sum(lf.reshape(n_chunks, STABLE_CHUNK_A, MXU_DIM), axis=1))
        return (1.0 - f).astype(BF16), lf, decay
    if section == 2:
        return a.astype(BF16)
    return jax.nn.silu(a)


def _hgrn_store_projection(pieces, q_s, k_s, v_s, lf_s, gate_s, stable_s):
    decay = None
    for tile in range(D_MODEL // MXU_DIM):
        k, lf, tile_decay = pieces[(1, tile)]
        decay = tile_decay if decay is None else jnp.minimum(decay, tile_decay)
        gate_s[:, tile * MXU_DIM:(tile + 1) * MXU_DIM] = pieces[(3, tile)]
        for i in range(HEADS_PER_TILE):
            h = tile * HEADS_PER_TILE + i
            ls = slice(i * DK_A, (i + 1) * DK_A)
            q_s[h] = pieces[(0, tile)][:, ls]
            k_s[h] = k[:, ls]
            v_s[h] = pieces[(2, tile)][:, ls]
            lf_s[h] = lf[:, ls]
    stable_s[0] = (decay >= -DECAY_LIMIT_LOG2).astype(jnp.int32)


def _hgrn_prompt_kernel(layer, x_ref, xnext_ref, lbraw_ref, win_ref, ng_ref, wout_ref, lng_ref,
                        lnb_ref, mask_ref, sign_ref, fwi_ref, fwo_ref, o_ref, sfin_ref, fwib_ref,
                        fwob_ref, st_ref, q_s, k_s, v_s, lf_s, oh_s, gate_s, stable_s):
    t = pl.program_id(1)
    _cast_ffn_slabs(fwi_ref, fwo_ref, fwib_ref, fwob_ref)
    lb = _lower_bound(lbraw_ref[...], layer)

    def project(x):
        xb = x.astype(BF16)
        return {st: _hgrn_project_tile(st[0], st[1], xb, win_ref, lb) for st in PROJ_TILES}

    def store(pieces):
        _hgrn_store_projection(pieces, q_s, k_s, v_s, lf_s, gate_s, stable_s)

    @pl.when((pl.program_id(0) == 0) & (t == 0))
    def _():
        store(project(x_ref[0]))

    @pl.when(t == 0)
    def _():
        st_ref[...] = jnp.zeros_like(st_ref)

    norm_g = ng_ref[...]

    def lower_triangle(n_rows):
        row = lax.broadcasted_iota(jnp.int32, (n_rows, n_rows), 0)
        col = lax.broadcasted_iota(jnp.int32, (n_rows, n_rows), 1)
        return row >= col

    def finish_rows(rows):
        o = jnp.concatenate([oh_s[h, rows, :] for h in range(H_A)], axis=1) * gate_s[rows, :]
        y = _dot(o.astype(BF16), wout_ref[...])
        o_ref[0, rows, :] = _layer_norm(ALPHA * x_ref[0, rows, :] + y, lng_ref[...], lnb_ref[...])

    def run_stable():
        causal = lower_triangle(STABLE_CHUNK_A)
        tri = causal.astype(BF16)
        causal = causal.astype(F32)
        heads = list(range(H_A))
        xb = xnext_ref[0].astype(BF16)
        todo = list(PROJ_TILES)
        pieces = {}
        share = iter(PROJ_TILES_PER_SLOT * len(STABLE_CHUNK_GROUPS))

        def between():
            for st in todo[:next(share)]:
                pieces[st] = _hgrn_project_tile(st[0], st[1], xb, win_ref, lb)
                todo.remove(st)

        c0 = 0
        for group in STABLE_CHUNK_GROUPS:
            _hgrn_stable_chunks(heads, list(range(c0, c0 + group)), tri, causal, q_s, k_s, v_s,
                                lf_s, oh_s, st_ref, norm_g, between)
            finish_rows(slice(c0 * STABLE_CHUNK_A, (c0 + group) * STABLE_CHUNK_A))
            between()
            c0 += group
        assert not todo
        store(pieces)

    def run_general():
        tri = lower_triangle(CHUNK_A).astype(BF16)

        def group_body(hg, carry):
            heads = [hg * HEAD_GROUP + i for i in range(HEAD_GROUP)]

            def chunk_body(c, carry2):
                _hgrn_general_chunk(heads, c, tri, q_s, k_s, v_s, lf_s, oh_s, st_ref, mask_ref,
                                    sign_ref, norm_g)
                return carry2
            return lax.fori_loop(0, ROW_TILE // CHUNK_A, chunk_body, carry)

        lax.fori_loop(0, H_A // HEAD_GROUP, group_body, 0)
        finish_rows(slice(0, ROW_TILE))
        store(project(xnext_ref[0]))

    lax.cond(stable_s[0] != 0, run_stable, run_general)

    @pl.when(t == pl.num_programs(1) - 1)
    def _():
        for h in range(H_A):
            sfin_ref[0, h] = st_ref[h].T


def _const_spec(shape):
    nd = len(shape)
    return pl.BlockSpec(shape, lambda *_: (0,) * nd, pipeline_mode=pl.Buffered(1))


def _layer_spec(shape, j):
    nd = len(shape)
    return pl.BlockSpec((None,) + tuple(shape[1:]), lambda *_: (j,) + (0,) * (nd - 1),
                        pipeline_mode=pl.Buffered(1))


def _ffn_cast_specs(ffn_w_in, ffn_w_out, layer, steps_per_row):
    _, d, ff2 = ffn_w_in.shape
    _, ff, _ = ffn_w_out.shape
    steps = FFN_CAST_STEPS
    rows_in, rows_out = d // steps, 2 * ff // steps
    assert d % steps == 0 and (2 * ff) % steps == 0
    assert rows_in % BF16_SUBLANES == 0 and rows_out % BF16_SUBLANES == 0

    def step(i, t):
        return i * steps_per_row + t

    in_specs = [
        pl.BlockSpec((None, rows_in, ff2), lambda i, t: (layer, step(i, t), 0)),
        pl.BlockSpec((None, rows_out, d), lambda i, t: (layer, step(i, t) // 2, 0)),
    ]
    out_specs = [
        pl.BlockSpec((rows_in, ff2), lambda i, t: (step(i, t), 0)),
        pl.BlockSpec((rows_out, d), lambda i, t: (step(i, t) // 2, 0)),
    ]
    out_shapes = [jax.ShapeDtypeStruct((d, ff2), BF16), jax.ShapeDtypeStruct((ff, d), BF16)]
    return in_specs, out_specs, out_shapes


def _cast_ffn_slabs(fwi_ref, fwo_ref, fwib_ref, fwob_ref):
    fwib_ref[...] = fwi_ref[...].astype(BF16)
    fwob_ref[...] = fwo_ref[...].astype(BF16)


def _params(n_axes):
    return pltpu.CompilerParams(dimension_semantics=("arbitrary",) * n_axes,
                                vmem_limit_bytes=VMEM_LIMIT_BYTES)


def _hgrn_prompt(x, layer, lb_raw, w_in, norm_g, w_out, ln_g, ln_b, masks, signs, ffn_w_in,
                 ffn_w_out):
    b, l, d = x.shape
    j = layer // N_MIXERS
    assert b * (l // ROW_TILE) == FFN_CAST_STEPS
    cast_in, cast_out, cast_shapes = _ffn_cast_specs(ffn_w_in, ffn_w_out, layer, l // ROW_TILE)
    assert l % ROW_TILE == 0 and ROW_TILE % CHUNK_A == 0 and H_A % HEAD_GROUP == 0
    assert sum(STABLE_CHUNK_GROUPS) * STABLE_CHUNK_A == ROW_TILE
    nt = l // ROW_TILE

    def next_tile(i, t):
        last = (i == b - 1) & (t == nt - 1)
        wrap = t == nt - 1
        return (jnp.where(wrap & ~last, i + 1, i), jnp.where(last, t, jnp.where(wrap, 0, t + 1)), 0)

    head_operand = pltpu.VMEM((H_A, ROW_TILE, DK_A), BF16)
    return pl.pallas_call(
        functools.partial(_hgrn_prompt_kernel, layer),
        grid=(b, l // ROW_TILE),
        in_specs=[
            pl.BlockSpec((1, ROW_TILE, d), lambda i, t: (i, t, 0)),
            pl.BlockSpec((1, ROW_TILE, d), next_tile),
            _const_spec(lb_raw.shape),
            _layer_spec(w_in.shape, j),
            _layer_spec(norm_g.shape, j),
            _layer_spec(w_out.shape, j),
            _layer_spec(ln_g.shape, layer),
            _layer_spec(ln_b.shape, layer),
            _const_spec(masks.shape),
            _const_spec(signs.shape),
        ] + cast_in,
        out_specs=[
            pl.BlockSpec((1, ROW_TILE, d), lambda i, t: (i, t, 0)),
            pl.BlockSpec((1, H_A, DK_A, DV_A), lambda i, t: (i, 0, 0, 0)),
        ] + cast_out,
        out_shape=[
            jax.ShapeDtypeStruct((b, l, d), F32),
            jax.ShapeDtypeStruct((b, H_A, DK_A, DV_A), F32),
        ] + cast_shapes,
        scratch_shapes=[
            pltpu.VMEM((H_A, DV_A, DK_A), F32),
            head_operand, head_operand, head_operand,
            pltpu.VMEM((H_A, ROW_TILE, DK_A), F32),
            pltpu.VMEM((H_A, ROW_TILE, DV_A), F32),
            pltpu.VMEM((ROW_TILE, d), F32),
            pltpu.SMEM((1,), jnp.int32),
        ],
        compiler_params=_params(2),
        name="hgrn_prompt",
    )(x, x, lb_raw, w_in, norm_g, w_out, ln_g, ln_b, masks, signs, ffn_w_in, ffn_w_out)


def _split3(f):
    hi = f.astype(BF16).astype(F32)
    r = f - hi
    mid = r.astype(BF16).astype(F32)
    lo = (r - mid).astype(BF16).astype(F32)
    return hi, mid, lo


def _hgrn_decode_kernel(layer, chained, x_ref, lbraw_ref, win_ref, ng_ref, wout_ref, lng_ref,
                        lnb_ref, s_ref, *rest):
    o_ref, snew_ref, f3_s, k_s, q_s, v_s, gate_s, oacc_s = rest[1:] if chained else rest

    @pl.when(pl.program_id(0) == 0)
    def _():
        _hgrn_decode_step(layer, x_ref, lbraw_ref, win_ref, ng_ref, wout_ref, lng_ref, lnb_ref,
                          s_ref, o_ref, snew_ref, f3_s, k_s, q_s, v_s, gate_s, oacc_s)

    @pl.when(pl.program_id(0) > 0)
    def _():
        snew_ref[...] = jnp.zeros_like(snew_ref)


def _hgrn_decode_step(layer, x_ref, lbraw_ref, win_ref, ng_ref, wout_ref, lng_ref, lnb_ref,
                      s_ref, o_ref, snew_ref, f3_s, k_s, q_s, v_s, gate_s, oacc_s):
    step = pl.program_id(1)
    d = D_MODEL
    nb = DEC_SEQS

    @pl.when(step == 0)
    def _():
        q, f, v, gate = _hgrn_gates(x_ref[...].astype(BF16), win_ref,
                                    _lower_bound(lbraw_ref[...], layer))
        hi, mid, lo = _split3(f)
        f3_s[0] = hi
        f3_s[1] = mid
        f3_s[2] = lo
        k_s[...] = 1.0 - f
        q_s[...] = q
        v_s[...] = v
        gate_s[...] = gate

    rows = pl.ds(pl.multiple_of(step * nb, nb), nb)
    fhi = f3_s[0, rows, :]
    fmid = f3_s[1, rows, :]
    flo = f3_s[2, rows, :]
    k8 = k_s[rows, :]
    q8 = q_s[rows, :]
    v8 = v_s[rows, :]
    ones = jnp.ones((nb, DV_A), F32)
    zeros = jnp.zeros((nb, DV_A), F32)
    rhs = []
    for h in range(H_A):
        ls = slice(h * DK_A, (h + 1) * DK_A)
        left = jnp.concatenate([ones, ones, ones, zeros], axis=0)
        right = jnp.concatenate([zeros, zeros, zeros, v8[:, ls]], axis=0)
        rhs.append(jnp.concatenate([left, right], axis=1).astype(BF16))
    seq = lax.broadcasted_iota(jnp.int32, (nb, DK_A), 0)
    head_lanes = [slice(h * DK_A, (h + 1) * DK_A) for h in range(H_A)]

    def group_body(p, o8):
        seqs = [p * DEC_SEQ_GROUP + i for i in range(DEC_SEQ_GROUP)]
        sels = [seq == i for i in seqs]
        chains = [(si, h) for si in range(DEC_SEQ_GROUP) for h in range(H_A)]
        s_old = [s_ref[seqs[si], h] for si, h in chains]
        lhs = [jnp.concatenate([jnp.where(sels[si], a[:, head_lanes[h]], 0.0)
                                for a in (fhi, fmid, flo, k8)], axis=0).astype(BF16)
               for si, h in chains]
        fb_kv = [_dot_tn(lhs[c], rhs[h]) for c, (si, h) in enumerate(chains)]
        s_new = [fb_kv[c][:, :DV_A] * s_old[c] + fb_kv[c][:, DV_A:] for c in range(len(chains))]
        qsel = [jnp.where(sels[si], q8[:, head_lanes[h]], 0.0).astype(BF16) for si, h in chains]
        outs = [_dot(qsel[c], s_new[c].astype(BF16)) for c in range(len(chains))]
        for c, (si, h) in enumerate(chains):
            snew_ref[seqs[si], h] = s_new[c]
        for si in range(DEC_SEQ_GROUP):
            o8 = o8 + jnp.concatenate(outs[si * H_A:(si + 1) * H_A], axis=1)
        return o8

    oacc_s[rows, :] = lax.fori_loop(0, nb // DEC_SEQ_GROUP, group_body, jnp.zeros((nb, d), F32))

    @pl.when(step == pl.num_programs(1) - 1)
    def _():
        o = oacc_s[...]
        norm_g = ng_ref[...]
        heads = []
        for h in range(H_A):
            oh = o[:, h * DV_A:(h + 1) * DV_A]
            ms = jnp.mean(oh * oh, axis=-1, keepdims=True)
            heads.append(oh * lax.rsqrt(ms + RMS_EPS) * norm_g)
        on = jnp.concatenate(heads, axis=1) * gate_s[...]
        y = _dot(on.astype(BF16), wout_ref[...])
        o_ref[...] = _layer_norm(ALPHA * x_ref[...] + y, lng_ref[...], lnb_ref[...])


def _hgrn_decode(x, layer, lb_raw, w_in, norm_g, w_out, ln_g, ln_b, states, new_states):
    n, d = x.shape
    j = layer // N_MIXERS
    assert n % DEC_SEQS == 0 and DEC_SEQS % DEC_SEQ_GROUP == 0
    chained = new_states is not None
    steps = n // DEC_SEQS
    passes = 1 if chained else states.shape[0] - j
    row_scratch = pltpu.VMEM((n, d), F32)
    state_block = (None, DEC_SEQS, H_A, DK_A, DV_A)
    state_in = pl.BlockSpec(state_block,
                            lambda g, s: (j, jnp.where(g == 0, s, steps - 1), 0, 0, 0))
    state_out = pl.BlockSpec(state_block, lambda g, s: (j + g, s, 0, 0, 0))
    operands = [x, lb_raw, w_in, norm_g, w_out, ln_g, ln_b, states]
    in_specs = [
        _const_spec(x.shape),
        _const_spec(lb_raw.shape),
        _layer_spec(w_in.shape, j),
        _layer_spec(norm_g.shape, j),
        _layer_spec(w_out.shape, j),
        _layer_spec(ln_g.shape, layer),
        _layer_spec(ln_b.shape, layer),
        state_in,
    ]
    if chained:
        operands.append(new_states)
        in_specs.append(pl.BlockSpec(memory_space=pl.ANY))
    return pl.pallas_call(
        functools.partial(_hgrn_decode_kernel, layer, chained),
        grid=(passes, steps),
        in_specs=in_specs,
        out_specs=[pl.BlockSpec((n, d), lambda g, s: (0, 0)), state_out],
        out_shape=[
            jax.ShapeDtypeStruct((n, d), F32),
            jax.ShapeDtypeStruct(states.shape, F32),
        ],
        scratch_shapes=[
            pltpu.VMEM((3, n, d), F32),
            row_scratch, row_scratch, row_scratch, row_scratch, row_scratch,
        ],
        input_output_aliases={len(operands) - 1: 1} if chained else {},
        compiler_params=_params(2),
        name="hgrn_decode",
    )(*operands)


def _cmlp_gate_values(x, win_ref, vg_ref, vb_ref):
    xb = x.astype(BF16)
    di = D_INNER_B
    v = _layer_norm(_gelu(_dot(xb, win_ref[:, di:2 * di])), vg_ref[...], vb_ref[...])
    u = _gelu(_dot(xb, win_ref[:, 0:di]))
    return u, v


def _cmlp_finish(x, u, mixed, wout_ref, lng_ref, lnb_ref):
    y = _dot((u * mixed).astype(BF16), wout_ref[...])
    return _layer_norm(ALPHA * x + y, lng_ref[...], lnb_ref[...])


def _cmlp_kernel(x_ref, xdec_ref, win_ref, vg_ref, vb_ref, ws_ref, bias_ref, wout_ref,
                 lng_ref, lnb_ref, fwi_ref, fwo_ref, o_ref, vrows_ref, odec_ref, vdec_ref,
                 fwib_ref, fwob_ref):
    _cast_ffn_slabs(fwi_ref, fwo_ref, fwib_ref, fwob_ref)
    x = x_ref[...]
    n = x.shape[0]
    u, v = _cmlp_gate_values(x, win_ref, vg_ref, vb_ref)
    row = lax.broadcasted_iota(jnp.int32, (CHUNK_B, CHUNK_B), 0)
    col = lax.broadcasted_iota(jnp.int32, (CHUNK_B, CHUNK_B), 1)
    w = [jnp.where(row >= col, ws_ref[g], 0.0).astype(BF16) for g in range(G_B)]
    bias = bias_ref[...]
    vb16 = v.astype(BF16)
    chunks = []
    for c in range(n // CHUNK_B):
        rs = slice(c * CHUNK_B, (c + 1) * CHUNK_B)
        groups = [_dot(w[g], vb16[rs, g * DG_B:(g + 1) * DG_B]) for g in range(G_B)]
        chunks.append(jnp.concatenate(groups, axis=1) + bias)
    o_ref[...] = _cmlp_finish(x, u, jnp.concatenate(chunks, axis=0), wout_ref, lng_ref, lnb_ref)

    @pl.when(pl.program_id(1) == pl.num_programs(1) - 1)
    def _():
        vrows_ref[...] = v[n - CHUNK_B:, :]

    @pl.when((pl.program_id(0) == 0) & (pl.program_id(1) == 0))
    def _():
        xd = xdec_ref[...]
        ud, vd = _cmlp_gate_values(xd, win_ref, vg_ref, vb_ref)
        w00 = jnp.concatenate([jnp.broadcast_to(ws_ref[g, 0:1, 0:1], (1, DG_B))
                               for g in range(G_B)], axis=1)
        vdec_ref[...] = vd
        odec_ref[...] = _cmlp_finish(xd, ud, vd * w00 + bias[0:1, :], wout_ref, lng_ref, lnb_ref)


def _cmlp(x, x_dec, j, layer, w_in, v_g, v_b, w_s, bias_full, w_out, ln_g, ln_b, ffn_w_in,
          ffn_w_out):
    b, l, d = x.shape
    assert l % ROW_TILE == 0 and ROW_TILE % CHUNK_B == 0
    assert b * (l // ROW_TILE) == FFN_CAST_STEPS
    cast_in, cast_out, cast_shapes = _ffn_cast_specs(ffn_w_in, ffn_w_out, layer, l // ROW_TILE)
    dec_out = pl.BlockSpec(x_dec.shape, lambda i, t: (0, 0))
    return pl.pallas_call(
        _cmlp_kernel,
        grid=(b, l // ROW_TILE),
        in_specs=[
            pl.BlockSpec((None, ROW_TILE, d), lambda i, t: (i, t, 0)),
            _const_spec(x_dec.shape),
            _layer_spec(w_in.shape, j),
            _layer_spec(v_g.shape, j),
            _layer_spec(v_b.shape, j),
            _layer_spec(w_s.shape, j),
            _layer_spec(bias_full.shape, j),
            _layer_spec(w_out.shape, j),
            _layer_spec(ln_g.shape, layer),
            _layer_spec(ln_b.shape, layer),
        ] + cast_in,
        out_specs=[
            pl.BlockSpec((None, ROW_TILE, d), lambda i, t: (i, t, 0)),
            pl.BlockSpec((None, CHUNK_B, D_INNER_B), lambda i, t: (i, 0, 0)),
            dec_out,
            dec_out,
        ] + cast_out,
        out_shape=[
            jax.ShapeDtypeStruct((b, l, d), F32),
            jax.ShapeDtypeStruct((b, CHUNK_B, D_INNER_B), F32),
            jax.ShapeDtypeStruct(x_dec.shape, F32),
            jax.ShapeDtypeStruct(x_dec.shape, F32),
        ] + cast_shapes,
        compiler_params=_params(2),
        name="cmlp",
    )(x, x_dec, w_in, v_g, v_b, w_s, bias_full, w_out, ln_g, ln_b, ffn_w_in, ffn_w_out)


def _ffn_rows(x, win_ref, wout_ref, ln_g, ln_b):
    xb = x.astype(BF16)
    acc = jnp.zeros(x.shape, F32)
    for j in range(D_FF // FF_TILE):
        cs = slice(j * FF_TILE, (j + 1) * FF_TILE)
        us = slice(D_FF + j * FF_TILE, D_FF + (j + 1) * FF_TILE)
        hcol = jax.nn.silu(_dot(xb, win_ref[:, cs])) * _dot(xb, win_ref[:, us])
        acc = acc + _dot(hcol.astype(BF16), wout_ref[cs, :])
    return _layer_norm(ALPHA * x + acc, ln_g, ln_b)


def _ffn_kernel(x_ref, xdec_ref, win_ref, wout_ref, lng_ref, lnb_ref, o_ref, odec_ref):
    ln_g, ln_b = lng_ref[...], lnb_ref[...]
    half = x_ref.shape[0] // FFN_ROW_SPLIT
    for i in range(FFN_ROW_SPLIT):
        rs = slice(i * half, (i + 1) * half)
        o_ref[rs, :] = _ffn_rows(x_ref[rs, :], win_ref, wout_ref, ln_g, ln_b)

    @pl.when(pl.program_id(0) == 0)
    def _():
        odec_ref[...] = _ffn_rows(xdec_ref[...], win_ref, wout_ref, ln_g, ln_b)


def _ffn(x, x_dec, layer, w_in, w_out, ln_g, ln_b):
    n, d = x.shape
    assert n % FFN_ROW_TILE == 0 and D_FF % FF_TILE == 0
    return pl.pallas_call(
        _ffn_kernel,
        grid=(n // FFN_ROW_TILE,),
        in_specs=[
            pl.BlockSpec((FFN_ROW_TILE, d), lambda i: (i, 0)),
            _const_spec(x_dec.shape),
            _const_spec(w_in.shape),
            _const_spec(w_out.shape),
            _layer_spec(ln_g.shape, layer),
            _layer_spec(ln_b.shape, layer),
        ],
        out_specs=[
            pl.BlockSpec((FFN_ROW_TILE, d), lambda i: (i, 0)),
            pl.BlockSpec(x_dec.shape, lambda i: (0, 0)),
        ],
        out_shape=[
            jax.ShapeDtypeStruct((n, d), F32),
            jax.ShapeDtypeStruct(x_dec.shape, F32),
        ],
        compiler_params=_params(1),
        name="ffn",
    )(x, x_dec, w_in, w_out, ln_g, ln_b)


def kernel(x_prompt, x_sample, state_hgrn, ln_mix_g, ln_mix_b, ln_ffn_g, ln_ffn_b, a_lb_raw, a_w_in, a_norm_g, a_w_out, b_w_in, b_ln_g, b_ln_b, b_w_s, b_bias_s, b_w_out, ffn_w_in, ffn_w_out):
    bsz, seq, d = x_prompt.shape
    n_dec = x_sample.shape[0]
    assert x_sample.shape[1] == 1 and seq % CHUNK_B == 0

    masks = jnp.asarray(_level_masks(CHUNK_A))
    signs = jnp.asarray(_level_signs(CHUNK_A, DK_A))
    lb_raw = a_lb_raw.astype(F32)
    bias_full = jnp.repeat(jnp.swapaxes(b_bias_s, 1, 2), DG_B, axis=2)

    def rows(p):
        return p[:, None, :]

    mix_g, mix_b, ffn_g, ffn_b = rows(ln_mix_g), rows(ln_mix_b), rows(ln_ffn_g), rows(ln_ffn_b)
    norm_g, v_g, v_b = rows(a_norm_g), rows(b_ln_g), rows(b_ln_b)

    xp = x_prompt
    xs = x_sample.reshape(n_dec, d)
    st_p, v_p, v_s = [], [], []
    st_s = None
    for layer in range(DEPTH):
        j = layer // N_MIXERS
        if layer % N_MIXERS == 0:
            xp, s, fwi, fwo = _hgrn_prompt(xp, layer, lb_raw, a_w_in, norm_g, a_w_out, mix_g,
                                           mix_b, masks, signs, ffn_w_in, ffn_w_out)
            st_p.append(s)
            xs, st_s = _hgrn_decode(xs, layer, lb_raw, a_w_in, norm_g, a_w_out, mix_g, mix_b,
                                    state_hgrn, st_s)
        else:
            xp, v, xs, vd, fwi, fwo = _cmlp(xp, xs, j, layer, b_w_in, v_g, v_b, b_w_s, bias_full,
                                            b_w_out, mix_g, mix_b, ffn_w_in, ffn_w_out)
            v_p.append(v)
            v_s.append(vd[:, None, :])
        xp, xs = _ffn(xp.reshape(bsz * seq, d), xs, layer, fwi, fwo, ffn_g, ffn_b)
        xp = xp.reshape(bsz, seq, d)

    return (xp, xs.reshape(n_dec, 1, d), jnp.stack(st_p), st_s, jnp.stack(v_p), jnp.stack(v_s))
```

```python
import functools

import numpy as np
import jax
import jax.numpy as jnp
from jax import lax
from jax.experimental import pallas as pl
from jax.experimental.pallas import tpu as pltpu

D_MODEL = 1024
DEPTH = 4
N_MIXERS = 2
DK_A = 128
H_A = D_MODEL // DK_A
DV_A = D_MODEL // H_A
D_INNER_B = D_MODEL
CHUNK_B = 128
G_B = 8
DG_B = D_INNER_B // G_B
D_FF = -(-(8 * D_MODEL) // (3 * 256)) * 256
ALPHA = (2 * DEPTH) ** 0.25
LN_EPS = 1e-5
RMS_EPS = 1e-6

SUBLANES = 8
BF16_SUBLANES = 16
MXU_DIM = 256
VMEM_LIMIT_BYTES = 56 * 2**20

ROW_TILE = 512
FFN_ROW_TILE = 1024
FFN_ROW_SPLIT = 2
CHUNK_A = 128
HEAD_GROUP = 8
STABLE_CHUNK_A = 64
DECAY_LIMIT_LOG2 = 118.0
STABLE_CHUNK_GROUPS = (4, 4)
FF_TILE = MXU_DIM
FFN_CAST_STEPS = 32
DEC_SEQS = 8
DEC_STATE_BUFFERS = 3
DEC_SEQ_GROUP = 8

BF16 = jnp.bfloat16
F32 = jnp.float32


def _level_halves(chunk):
    out, h = [], chunk // 2
    while h >= 1:
        out.append(h)
        h //= 2
    return tuple(out)


LEVEL_HALVES = _level_halves(CHUNK_A)


def _level_masks(chunk):
    t = np.arange(chunk)[:, None]
    s = np.arange(chunk)[None, :]
    masks = [((t // (2 * h) == s // (2 * h)) & (t % (2 * h) >= h) & (s % (2 * h) < h))
             for h in _level_halves(chunk)]
    masks.append(t == s)
    return np.stack(masks).astype(np.float32)


def _level_signs(chunk, width):
    t = np.arange(chunk)[:, None]
    signs = [np.where(t % (2 * h) >= h, 1.0, -1.0) for h in _level_halves(chunk)]
    return np.broadcast_to(np.stack(signs), (len(signs), chunk, width)).astype(np.float32)


def _dot(a, b):
    return lax.dot_general(a, b, (((1,), (0,)), ((), ())), preferred_element_type=F32)


def _dot_nt(a, b):
    return lax.dot_general(a, b, (((1,), (1,)), ((), ())), preferred_element_type=F32)


def _dot_tn(a, b):
    return lax.dot_general(a, b, (((0,), (0,)), ((), ())), preferred_element_type=F32)


def _layer_norm(z, g, b):
    mu = jnp.mean(z, axis=-1, keepdims=True)
    zc = z - mu
    var = jnp.mean(zc * zc, axis=-1, keepdims=True)
    return zc * lax.rsqrt(var + LN_EPS) * g + b


def _gelu(x):
    return 0.5 * x * (1.0 + lax.erf(x * (2.0 ** -0.5)))


def _lower_bound(raw, layer):
    m = jnp.max(raw, axis=0, keepdims=True)
    e = jnp.exp(raw - m)
    p = e / jnp.sum(e, axis=0, keepdims=True)
    c = p[0:1]
    for i in range(1, layer + 1):
        c = c + p[i:i + 1]
    return c - p[0:1]


def _hgrn_gates(xb, win_ref, lb):
    d = D_MODEL
    f = lb + (1.0 - lb) * jax.nn.sigmoid(_dot(xb, win_ref[:, d:2 * d]))
    q = jax.nn.silu(_dot(xb, win_ref[:, 0:d])) * (DK_A ** -0.5)
    gate = jax.nn.silu(_dot(xb, win_ref[:, 3 * d:4 * d]))
    v = _dot(xb, win_ref[:, 2 * d:3 * d])
    return q, f, v, gate


def _split2(a):
    hi = a.astype(BF16)
    lo = (a - hi.astype(F32)).astype(BF16)
    return jnp.concatenate([hi, lo], axis=1)


def _level_ref(g, half):
    n, w = g.shape
    if half >= SUBLANES // 2:
        blk = 2 * half
        parts = [jnp.broadcast_to(g[b * blk + half - 1:b * blk + half, :], (blk, w))
                 for b in range(n // blk)]
        return parts[0] if len(parts) == 1 else jnp.concatenate(parts, axis=0)
    row = lax.broadcasted_iota(jnp.int32, g.shape, 0)
    if half == 2:
        m = row % 4
        return jnp.where(m == 0, pltpu.roll(g, n - 1, axis=0),
                         jnp.where(m == 1, g,
                                   jnp.where(m == 2, pltpu.roll(g, 1, axis=0),
                                             pltpu.roll(g, 2, axis=0))))
    assert half == 1
    return jnp.where(row % 2 == 0, g, pltpu.roll(g, 1, axis=0))


def _rms_norm_rows(o, norm_g):
    ms = jnp.mean(o * o, axis=-1, keepdims=True)
    return o * lax.rsqrt(ms + RMS_EPS) * norm_g


def _hgrn_stable_chunks(heads, chunks, tri, causal, q_s, k_s, v_s, lf_s, oh_s, st_ref, norm_g,
                        between):
    n_rows = STABLE_CHUNK_A
    items = [(c, h) for c in chunks for h in heads]
    n = range(len(items))
    rows = [slice(c * n_rows, (c + 1) * n_rows) for c, _ in items]
    q = [q_s[h, rows[i], :] for i, (_, h) in enumerate(items)]
    k = [k_s[h, rows[i], :] for i, (_, h) in enumerate(items)]
    vb = [v_s[h, rows[i], :] for i, (_, h) in enumerate(items)]
    g2 = [_dot(tri, _split2(lf_s[h, rows[i], :])) for i, (_, h) in enumerate(items)]
    between()
    g = [a[:, :DK_A] + a[:, DK_A:] for a in g2]
    g_last = [a[n_rows - 1:n_rows, :] for a in g]
    qp = [q[i] * jnp.exp2(g[i]).astype(BF16) for i in n]
    ks = [k[i] * jnp.exp2(g_last[i] - g[i]).astype(BF16) for i in n]
    kv = [_dot_tn(vb[i], ks[i]) for i in n]
    a = [causal * _dot_nt(qp[i], k[i] * jnp.exp2(-g[i]).astype(BF16)) for i in n]
    between()
    intra = [_dot(a[i].astype(BF16), vb[i]) for i in n]
    between()
    st = {h: st_ref[h] for h in heads}
    for i, (_, h) in enumerate(items):
        o = intra[i] + _dot_nt(qp[i], st[h].astype(BF16))
        oh_s[h, rows[i], :] = _rms_norm_rows(o, norm_g)
        st[h] = st[h] * jnp.exp2(g_last[i]) + kv[i]
    for h in heads:
        st_ref[h] = st[h]


def _hgrn_general_chunk(heads, c, tri, q_s, k_s, v_s, lf_s, oh_s, st_ref, mask_ref, sign_ref,
                        norm_g):
    rows = pl.ds(pl.multiple_of(c * CHUNK_A, CHUNK_A), CHUNK_A)
    n = range(len(heads))
    q = [q_s[h, rows, :] for h in heads]
    k = [k_s[h, rows, :] for h in heads]
    vb = [v_s[h, rows, :] for h in heads]
    st = [st_ref[h] for h in heads]
    g2 = [_dot(tri, _split2(lf_s[h, rows, :])) for h in heads]
    g = [a[:, :DK_A] + a[:, DK_A:] for a in g2]
    g_last = [a[CHUNK_A - 1:CHUNK_A, :] for a in g]
    o = [_dot_nt(q[i] * jnp.exp2(g[i]).astype(BF16), st[i].astype(BF16)) for i in n]
    ks = [k[i] * jnp.exp2(g_last[i] - g[i]).astype(BF16) for i in n]
    st_new = [st[i] * jnp.exp2(g_last[i]) + _dot_tn(vb[i], ks[i]) for i in n]
    diag = mask_ref[len(LEVEL_HALVES)]
    a = [diag * _dot_nt(q[i], k[i]) for i in n]
    for lvl, half in enumerate(LEVEL_HALVES):
        e = [jnp.exp2((g[i] - _level_ref(g[i], half)) * sign_ref[lvl]).astype(BF16) for i in n]
        p = [_dot_nt(q[i] * e[i], k[i] * e[i]) for i in n]
        a = [a[i] + mask_ref[lvl] * p[i] for i in n]
    o = [o[i] + _dot(a[i].astype(BF16), vb[i]) for i in n]
    for i, h in enumerate(heads):
        st_ref[h] = st_new[i]
        oh_s[h, rows, :] = _rms_norm_rows(o[i], norm_g)


PROJ_TILES = [(section, tile) for section in (1, 0, 3, 2) for tile in range(D_MODEL // MXU_DIM)]
HEADS_PER_TILE = MXU_DIM // DK_A
PROJ_TILES_PER_SLOT = (3, 2, 1, 2)


def _hgrn_project_tile(section, tile, xb, win_ref, lb):
    c0 = section * D_MODEL + tile * MXU_DIM
    a = _dot(xb, win_ref[:, c0:c0 + MXU_DIM])
    if section == 0:
        return (jax.nn.silu(a) * (DK_A ** -0.5)).astype(BF16)
    if section == 1:
        lbt = lb[:, tile * MXU_DIM:(tile + 1) * MXU_DIM]
        f = lbt + (1.0 - lbt) * jax.nn.sigmoid(a)
        lf = jnp.log2(f)
        n_chunks = lf.shape[0] // STABLE_CHUNK_A
        decay = jnp.min(jnp.sum(lf.reshape(n_chunks, STABLE_CHUNK_A, MXU_DIM), axis=1))
        return (1.0 - f).astype(BF16), lf, decay
    if section == 2:
        return a.astype(BF16)
    return jax.nn.silu(a)


def _hgrn_store_projection(pieces, q_s, k_s, v_s, lf_s, gate_s, stable_s):
    decay = None
    for tile in range(D_MODEL // MXU_DIM):
        k, lf, tile_decay = pieces[(1, tile)]
        decay = tile_decay if decay is None else jnp.minimum(decay, tile_decay)
        gate_s[:, tile * MXU_DIM:(tile + 1) * MXU_DIM] = pieces[(3, tile)]
        for i in range(HEADS_PER_TILE):
            h = tile * HEADS_PER_TILE + i
            ls = slice(i * DK_A, (i + 1) * DK_A)
            q_s[h] = pieces[(0, tile)][:, ls]
            k_s[h] = k[:, ls]
            v_s[h] = pieces[(2, tile)][:, ls]
            lf_s[h] = lf[:, ls]
    stable_s[0] = (decay >= -DECAY_LIMIT_LOG2).astype(jnp.int32)


def _hgrn_prompt_kernel(layer, x_ref, xnext_ref, lbraw_ref, win_ref, ng_ref, wout_ref, lng_ref,
                        lnb_ref, mask_ref, sign_ref, fwi_ref, fwo_ref, o_ref, sfin_ref, fwib_ref,
                        fwob_ref, st_ref, q_s, k_s, v_s, lf_s, oh_s, gate_s, stable_s):
    t = pl.program_id(1)
    _cast_ffn_slabs(fwi_ref, fwo_ref, fwib_ref, fwob_ref)
    lb = _lower_bound(lbraw_ref[...], layer)

    def project(x):
        xb = x.astype(BF16)
        return {st: _hgrn_project_tile(st[0], st[1], xb, win_ref, lb) for st in PROJ_TILES}

    def store(pieces):
        _hgrn_store_projection(pieces, q_s, k_s, v_s, lf_s, gate_s, stable_s)

    @pl.when((pl.program_id(0) == 0) & (t == 0))
    def _():
        store(project(x_ref[0]))

    @pl.when(t == 0)
    def _():
        st_ref[...] = jnp.zeros_like(st_ref)

    norm_g = ng_ref[...]

    def lower_triangle(n_rows):
        row = lax.broadcasted_iota(jnp.int32, (n_rows, n_rows), 0)
        col = lax.broadcasted_iota(jnp.int32, (n_rows, n_rows), 1)
        return row >= col

    def finish_rows(rows):
        o = jnp.concatenate([oh_s[h, rows, :] for h in range(H_A)], axis=1) * gate_s[rows, :]
        y = _dot(o.astype(BF16), wout_ref[...])
        o_ref[0, rows, :] = _layer_norm(ALPHA * x_ref[0, rows, :] + y, lng_ref[...], lnb_ref[...])

    def run_stable():
        causal = lower_triangle(STABLE_CHUNK_A)
        tri = causal.astype(BF16)
        causal = causal.astype(F32)
        heads = list(range(H_A))
        xb = xnext_ref[0].astype(BF16)
        todo = list(PROJ_TILES)
        pieces = {}
        share = iter(PROJ_TILES_PER_SLOT * len(STABLE_CHUNK_GROUPS))

        def between():
            for st in todo[:next(share)]:
                pieces[st] = _hgrn_project_tile(st[0], st[1], xb, win_ref, lb)
                todo.remove(st)

        c0 = 0
        for group in STABLE_CHUNK_GROUPS:
            _hgrn_stable_chunks(heads, list(range(c0, c0 + group)), tri, causal, q_s, k_s, v_s,
                                lf_s, oh_s, st_ref, norm_g, between)
            finish_rows(slice(c0 * STABLE_CHUNK_A, (c0 + group) * STABLE_CHUNK_A))
            between()
            c0 += group
        assert not todo
        store(pieces)

    def run_general():
        tri = lower_triangle(CHUNK_A).astype(BF16)

        def group_body(hg, carry):
            heads = [hg * HEAD_GROUP + i for i in range(HEAD_GROUP)]

            def chunk_body(c, carry2):
                _hgrn_general_chunk(heads, c, tri, q_s, k_s, v_s, lf_s, oh_s, st_ref, mask_ref,
                                    sign_ref, norm_g)
                return carry2
            return lax.fori_loop(0, ROW_TILE // CHUNK_A, chunk_body, carry)

        lax.fori_loop(0, H_A // HEAD_GROUP, group_body, 0)
        finish_rows(slice(0, ROW_TILE))
        store(project(xnext_ref[0]))

    lax.cond(stable_s[0] != 0, run_stable, run_general)

    @pl.when(t == pl.num_programs(1) - 1)
    def _():
        for h in range(H_A):
            sfin_ref[0, h] = st_ref[h].T


def _const_spec(shape):
    nd = len(shape)
    return pl.BlockSpec(shape, lambda *_: (0,) * nd, pipeline_mode=pl.Buffered(1))


def _layer_spec(shape, j):
    nd = len(shape)
    return pl.BlockSpec((None,) + tuple(shape[1:]), lambda *_: (j,) + (0,) * (nd - 1),
                        pipeline_mode=pl.Buffered(1))


def _ffn_cast_specs(ffn_w_in, ffn_w_out, layer, steps_per_row):
    _, d, ff2 = ffn_w_in.shape
    _, ff, _ = ffn_w_out.shape
    steps = FFN_CAST_STEPS
    rows_in, rows_out = d // steps, 2 * ff // steps
    assert d % steps == 0 and (2 * ff) % steps == 0
    assert rows_in % BF16_SUBLANES == 0 and rows_out % BF16_SUBLANES == 0

    def step(i, t):
        return i * steps_per_row + t

    in_specs = [
        pl.BlockSpec((None, rows_in, ff2), lambda i, t: (layer, step(i, t), 0)),
        pl.BlockSpec((None, rows_out, d), lambda i, t: (layer, step(i, t) // 2, 0)),
    ]
    out_specs = [
        pl.BlockSpec((rows_in, ff2), lambda i, t: (step(i, t), 0)),
        pl.BlockSpec((rows_out, d), lambda i, t: (step(i, t) // 2, 0)),
    ]
    out_shapes = [jax.ShapeDtypeStruct((d, ff2), BF16), jax.ShapeDtypeStruct((ff, d), BF16)]
    return in_specs, out_specs, out_shapes


def _cast_ffn_slabs(fwi_ref, fwo_ref, fwib_ref, fwob_ref):
    fwib_ref[...] = fwi_ref[...].astype(BF16)
    fwob_ref[...] = fwo_ref[...].astype(BF16)


def _params(n_axes):
    return pltpu.CompilerParams(dimension_semantics=("arbitrary",) * n_axes,
                                vmem_limit_bytes=VMEM_LIMIT_BYTES)


def _hgrn_prompt(x, layer, lb_raw, w_in, norm_g, w_out, ln_g, ln_b, masks, signs, ffn_w_in,
                 ffn_w_out):
    b, l, d = x.shape
    j = layer // N_MIXERS
    assert b * (l // ROW_TILE) == FFN_CAST_STEPS
    cast_in, cast_out, cast_shapes = _ffn_cast_specs(ffn_w_in, ffn_w_out, layer, l // ROW_TILE)
    assert l % ROW_TILE == 0 and ROW_TILE % CHUNK_A == 0 and H_A % HEAD_GROUP == 0
    assert sum(STABLE_CHUNK_GROUPS) * STABLE_CHUNK_A == ROW_TILE
    nt = l // ROW_TILE

    def next_tile(i, t):
        last = (i == b - 1) & (t == nt - 1)
        wrap = t == nt - 1
        return (jnp.where(wrap & ~last, i + 1, i), jnp.where(last, t, jnp.where(wrap, 0, t + 1)), 0)

    head_operand = pltpu.VMEM((H_A, ROW_TILE, DK_A), BF16)
    return pl.pallas_call(
        functools.partial(_hgrn_prompt_kernel, layer),
        grid=(b, l // ROW_TILE),
        in_specs=[
            pl.BlockSpec((1, ROW_TILE, d), lambda i, t: (i, t, 0)),
            pl.BlockSpec((1, ROW_TILE, d), next_tile),
            _const_spec(lb_raw.shape),
            _layer_spec(w_in.shape, j),
            _layer_spec(norm_g.shape, j),
            _layer_spec(w_out.shape, j),
            _layer_spec(ln_g.shape, layer),
            _layer_spec(ln_b.shape, layer),
            _const_spec(masks.shape),
            _const_spec(signs.shape),
        ] + cast_in,
        out_specs=[
            pl.BlockSpec((1, ROW_TILE, d), lambda i, t: (i, t, 0)),
            pl.BlockSpec((1, H_A, DK_A, DV_A), lambda i, t: (i, 0, 0, 0)),
        ] + cast_out,
        out_shape=[
            jax.ShapeDtypeStruct((b, l, d), F32),
            jax.ShapeDtypeStruct((b, H_A, DK_A, DV_A), F32),
        ] + cast_shapes,
        scratch_shapes=[
            pltpu.VMEM((H_A, DV_A, DK_A), F32),
            head_operand, head_operand, head_operand,
            pltpu.VMEM((H_A, ROW_TILE, DK_A), F32),
            pltpu.VMEM((H_A, ROW_TILE, DV_A), F32),
            pltpu.VMEM((ROW_TILE, d), F32),
            pltpu.SMEM((1,), jnp.int32),
        ],
        compiler_params=_params(2),
        name="hgrn_prompt",
    )(x, x, lb_raw, w_in, norm_g, w_out, ln_g, ln_b, masks, signs, ffn_w_in, ffn_w_out)


def _split3(f):
    hi = f.astype(BF16).astype(F32)
    r = f - hi
    mid = r.astype(BF16).astype(F32)
    lo = (r - mid).astype(BF16).astype(F32)
    return hi, mid, lo


def _hgrn_decode_kernel(layer, chained, x_ref, lbraw_ref, win_ref, ng_ref, wout_ref, lng_ref,
                        lnb_ref, s_ref, *rest):
    (o_ref, snew_ref, f3_s, k_s, q_s, v_s, gate_s, oacc_s, sbuf_s,
     sem) = rest[1:] if chained else rest

    @pl.when(pl.program_id(0) == 0)
    def _():
        _hgrn_decode_step(layer, x_ref, lbraw_ref, win_ref, ng_ref, wout_ref, lng_ref, lnb_ref,
                          s_ref, o_ref, snew_ref, f3_s, k_s, q_s, v_s, gate_s, oacc_s, sbuf_s, sem)

    @pl.when(pl.program_id(0) > 0)
    def _():
        snew_ref[...] = jnp.zeros_like(snew_ref)


def _hgrn_decode_step(layer, x_ref, lbraw_ref, win_ref, ng_ref, wout_ref, lng_ref, lnb_ref,
                      s_hbm, o_ref, snew_ref, f3_s, k_s, q_s, v_s, gate_s, oacc_s, sbuf_s, sem):
    step = pl.program_id(1)
    steps = pl.num_programs(1)
    d = D_MODEL
    nb = DEC_SEQS
    ahead = DEC_STATE_BUFFERS - 1
    j = layer // N_MIXERS

    def state_copy(blk):
        slot = blk % DEC_STATE_BUFFERS
        return pltpu.make_async_copy(s_hbm.at[j, pl.ds(blk * nb, nb)], sbuf_s.at[slot], sem.at[slot])

    @pl.when(step == 0)
    def _():
        for blk in range(ahead):
            state_copy(blk).start()

    @pl.when(step + ahead < steps)
    def _():
        state_copy(step + ahead).start()

    @pl.when(step == 0)
    def _():
        q, f, v, gate = _hgrn_gates(x_ref[...].astype(BF16), win_ref,
                                    _lower_bound(lbraw_ref[...], layer))
        hi, mid, lo = _split3(f)
        f3_s[0] = hi
        f3_s[1] = mid
        f3_s[2] = lo
        k_s[...] = 1.0 - f
        q_s[...] = q
        v_s[...] = v
        gate_s[...] = gate

    rows = pl.ds(pl.multiple_of(step * nb, nb), nb)
    fhi = f3_s[0, rows, :]
    fmid = f3_s[1, rows, :]
    flo = f3_s[2, rows, :]
    k8 = k_s[rows, :]
    q8 = q_s[rows, :]
    v8 = v_s[rows, :]
    ones = jnp.ones((nb, DV_A), F32)
    zeros = jnp.zeros((nb, DV_A), F32)
    rhs = []
    for h in range(H_A):
        ls = slice(h * DK_A, (h + 1) * DK_A)
        left = jnp.concatenate([ones, ones, ones, zeros], axis=0)
        right = jnp.concatenate([zeros, zeros, zeros, v8[:, ls]], axis=0)
        rhs.append(jnp.concatenate([left, right], axis=1).astype(BF16))
    seq = lax.broadcasted_iota(jnp.int32, (nb, DK_A), 0)
    head_lanes = [slice(h * DK_A, (h + 1) * DK_A) for h in range(H_A)]
    state_copy(step).wait()
    s_ref = sbuf_s.at[step % DEC_STATE_BUFFERS]

    def group_body(p, o8):
        seqs = [p * DEC_SEQ_GROUP + i for i in range(DEC_SEQ_GROUP)]
        sels = [seq == i for i in seqs]
        chains = [(si, h) for si in range(DEC_SEQ_GROUP) for h in range(H_A)]
        s_old = [s_ref[seqs[si], h] for si, h in chains]
        lhs = [jnp.concatenate([jnp.where(sels[si], a[:, head_lanes[h]], 0.0)
                                for a in (fhi, fmid, flo, k8)], axis=0).astype(BF16)
               for si, h in chains]
        fb_kv = [_dot_tn(lhs[c], rhs[h]) for c, (si, h) in enumerate(chains)]
        s_new = [fb_kv[c][:, :DV_A] * s_old[c] + fb_kv[c][:, DV_A:] for c in range(len(chains))]
        qsel = [jnp.where(sels[si], q8[:, head_lanes[h]], 0.0).astype(BF16) for si, h in chains]
        outs = [_dot(qsel[c], s_new[c].astype(BF16)) for c in range(len(chains))]
        for c, (si, h) in enumerate(chains):
            snew_ref[seqs[si], h] = s_new[c]
        for si in range(DEC_SEQ_GROUP):
            o8 = o8 + jnp.concatenate(outs[si * H_A:(si + 1) * H_A], axis=1)
        return o8

    oacc_s[rows, :] = lax.fori_loop(0, nb // DEC_SEQ_GROUP, group_body, jnp.zeros((nb, d), F32))

    @pl.when(step == pl.num_programs(1) - 1)
    def _():
        o = oacc_s[...]
        norm_g = ng_ref[...]
        heads = []
        for h in range(H_A):
            oh = o[:, h * DV_A:(h + 1) * DV_A]
            ms = jnp.mean(oh * oh, axis=-1, keepdims=True)
            heads.append(oh * lax.rsqrt(ms + RMS_EPS) * norm_g)
        on = jnp.concatenate(heads, axis=1) * gate_s[...]
        y = _dot(on.astype(BF16), wout_ref[...])
        o_ref[...] = _layer_norm(ALPHA * x_ref[...] + y, lng_ref[...], lnb_ref[...])


def _hgrn_decode(x, layer, lb_raw, w_in, norm_g, w_out, ln_g, ln_b, states, new_states):
    n, d = x.shape
    j = layer // N_MIXERS
    assert n % DEC_SEQS == 0 and DEC_SEQS % DEC_SEQ_GROUP == 0
    chained = new_states is not None
    steps = n // DEC_SEQS
    passes = 1 if chained else states.shape[0] - j
    row_scratch = pltpu.VMEM((n, d), F32)
    state_block = (None, DEC_SEQS, H_A, DK_A, DV_A)
    assert steps >= DEC_STATE_BUFFERS
    state_in = pl.BlockSpec(memory_space=pl.ANY)
    state_out = pl.BlockSpec(state_block, lambda g, s: (j + g, s, 0, 0, 0))
    operands = [x, lb_raw, w_in, norm_g, w_out, ln_g, ln_b, states]
    in_specs = [
        _const_spec(x.shape),
        _const_spec(lb_raw.shape),
        _layer_spec(w_in.shape, j),
        _layer_spec(norm_g.shape, j),
        _layer_spec(w_out.shape, j),
        _layer_spec(ln_g.shape, layer),
        _layer_spec(ln_b.shape, layer),
        state_in,
    ]
    if chained:
        operands.append(new_states)
        in_specs.append(pl.BlockSpec(memory_space=pl.ANY))
    return pl.pallas_call(
        functools.partial(_hgrn_decode_kernel, layer, chained),
        grid=(passes, steps),
        in_specs=in_specs,
        out_specs=[pl.BlockSpec((n, d), lambda g, s: (0, 0)), state_out],
        out_shape=[
            jax.ShapeDtypeStruct((n, d), F32),
            jax.ShapeDtypeStruct(states.shape, F32),
        ],
        scratch_shapes=[
            pltpu.VMEM((3, n, d), F32),
            row_scratch, row_scratch, row_scratch, row_scratch, row_scratch,
            pltpu.VMEM((DEC_STATE_BUFFERS, DEC_SEQS, H_A, DK_A, DV_A), F32),
            pltpu.SemaphoreType.DMA((DEC_STATE_BUFFERS,)),
        ],
        input_output_aliases={len(operands) - 1: 1} if chained else {},
        compiler_params=_params(2),
        name="hgrn_decode",
    )(*operands)


def _cmlp_gate_values(x, win_ref, vg_ref, vb_ref):
    xb = x.astype(BF16)
    di = D_INNER_B
    v = _layer_norm(_gelu(_dot(xb, win_ref[:, di:2 * di])), vg_ref[...], vb_ref[...])
    u = _gelu(_dot(xb, win_ref[:, 0:di]))
    return u, v


def _cmlp_finish(x, u, mixed, wout_ref, lng_ref, lnb_ref):
    y = _dot((u * mixed).astype(BF16), wout_ref[...])
    return _layer_norm(ALPHA * x + y, lng_ref[...], lnb_ref[...])


def _cmlp_kernel(x_ref, xdec_ref, win_ref, vg_ref, vb_ref, ws_ref, bias_ref, wout_ref,
                 lng_ref, lnb_ref, fwi_ref, fwo_ref, o_ref, vrows_ref, odec_ref, vdec_ref,
                 fwib_ref, fwob_ref):
    _cast_ffn_slabs(fwi_ref, fwo_ref, fwib_ref, fwob_ref)
    x = x_ref[...]
    n = x.shape[0]
    u, v = _cmlp_gate_values(x, win_ref, vg_ref, vb_ref)
    row = lax.broadcasted_iota(jnp.int32, (CHUNK_B, CHUNK_B), 0)
    col = lax.broadcasted_iota(jnp.int32, (CHUNK_B, CHUNK_B), 1)
    w = [jnp.where(row >= col, ws_ref[g], 0.0).astype(BF16) for g in range(G_B)]
    bias = bias_ref[...]
    vb16 = v.astype(BF16)
    chunks = []
    for c in range(n // CHUNK_B):
        rs = slice(c * CHUNK_B, (c + 1) * CHUNK_B)
        groups = [_dot(w[g], vb16[rs, g * DG_B:(g + 1) * DG_B]) for g in range(G_B)]
        chunks.append(jnp.concatenate(groups, axis=1) + bias)
    o_ref[...] = _cmlp_finish(x, u, jnp.concatenate(chunks, axis=0), wout_ref, lng_ref, lnb_ref)

    @pl.when(pl.program_id(1) == pl.num_programs(1) - 1)
    def _():
        vrows_ref[...] = v[n - CHUNK_B:, :]

    @pl.when((pl.program_id(0) == 0) & (pl.program_id(1) == 0))
    def _():
        xd = xdec_ref[...]
        ud, vd = _cmlp_gate_values(xd, win_ref, vg_ref, vb_ref)
        w00 = jnp.concatenate([jnp.broadcast_to(ws_ref[g, 0:1, 0:1], (1, DG_B))
                               for g in range(G_B)], axis=1)
        vdec_ref[...] = vd
        odec_ref[...] = _cmlp_finish(xd, ud, vd * w00 + bias[0:1, :], wout_ref, lng_ref, lnb_ref)


def _cmlp(x, x_dec, j, layer, w_in, v_g, v_b, w_s, bias_full, w_out, ln_g, ln_b, ffn_w_in,
          ffn_w_out):
    b, l, d = x.shape
    assert l % ROW_TILE == 0 and ROW_TILE % CHUNK_B == 0
    assert b * (l // ROW_TILE) == FFN_CAST_STEPS
    cast_in, cast_out, cast_shapes = _ffn_cast_specs(ffn_w_in, ffn_w_out, layer, l // ROW_TILE)
    dec_out = pl.BlockSpec(x_dec.shape, lambda i, t: (0, 0))
    return pl.pallas_call(
        _cmlp_kernel,
        grid=(b, l // ROW_TILE),
        in_specs=[
            pl.BlockSpec((None, ROW_TILE, d), lambda i, t: (i, t, 0)),
            _const_spec(x_dec.shape),
            _layer_spec(w_in.shape, j),
            _layer_spec(v_g.shape, j),
            _layer_spec(v_b.shape, j),
            _layer_spec(w_s.shape, j),
            _layer_spec(bias_full.shape, j),
            _layer_spec(w_out.shape, j),
            _layer_spec(ln_g.shape, layer),
            _layer_spec(ln_b.shape, layer),
        ] + cast_in,
        out_specs=[
            pl.BlockSpec((None, ROW_TILE, d), lambda i, t: (i, t, 0)),
            pl.BlockSpec((None, CHUNK_B, D_INNER_B), lambda i, t: (i, 0, 0)),
            dec_out,
            dec_out,
        ] + cast_out,
        out_shape=[
            jax.ShapeDtypeStruct((b, l, d), F32),
            jax.ShapeDtypeStruct((b, CHUNK_B, D_INNER_B), F32),
            jax.ShapeDtypeStruct(x_dec.shape, F32),
            jax.ShapeDtypeStruct(x_dec.shape, F32),
        ] + cast_shapes,
        compiler_params=_params(2),
        name="cmlp",
    )(x, x_dec, w_in, v_g, v_b, w_s, bias_full, w_out, ln_g, ln_b, ffn_w_in, ffn_w_out)


def _ffn_rows(x, win_ref, wout_ref, ln_g, ln_b):
    xb = x.astype(BF16)
    acc = jnp.zeros(x.shape, F32)
    for j in range(D_FF // FF_TILE):
        cs = slice(j * FF_TILE, (j + 1) * FF_TILE)
        us = slice(D_FF + j * FF_TILE, D_FF + (j + 1) * FF_TILE)
        hcol = jax.nn.silu(_dot(xb, win_ref[:, cs])) * _dot(xb, win_ref[:, us])
        acc = acc + _dot(hcol.astype(BF16), wout_ref[cs, :])
    return _layer_norm(ALPHA * x + acc, ln_g, ln_b)


def _ffn_kernel(x_ref, xdec_ref, win_ref, wout_ref, lng_ref, lnb_ref, o_ref, odec_ref):
    ln_g, ln_b = lng_ref[...], lnb_ref[...]
    half = x_ref.shape[0] // FFN_ROW_SPLIT
    for i in range(FFN_ROW_SPLIT):
        rs = slice(i * half, (i + 1) * half)
        o_ref[rs, :] = _ffn_rows(x_ref[rs, :], win_ref, wout_ref, ln_g, ln_b)

    @pl.when(pl.program_id(0) == 0)
    def _():
        odec_ref[...] = _ffn_rows(xdec_ref[...], win_ref, wout_ref, ln_g, ln_b)


def _ffn(x, x_dec, layer, w_in, w_out, ln_g, ln_b):
    n, d = x.shape
    assert n % FFN_ROW_TILE == 0 and D_FF % FF_TILE == 0
    return pl.pallas_call(
        _ffn_kernel,
        grid=(n // FFN_ROW_TILE,),
        in_specs=[
            pl.BlockSpec((FFN_ROW_TILE, d), lambda i: (i, 0)),
            _const_spec(x_dec.shape),
            _const_spec(w_in.shape),
            _const_spec(w_out.shape),
            _layer_spec(ln_g.shape, layer),
            _layer_spec(ln_b.shape, layer),
        ],
        out_specs=[
            pl.BlockSpec((FFN_ROW_TILE, d), lambda i: (i, 0)),
            pl.BlockSpec(x_dec.shape, lambda i: (0, 0)),
        ],
        out_shape=[
            jax.ShapeDtypeStruct((n, d), F32),
            jax.ShapeDtypeStruct(x_dec.shape, F32),
        ],
        compiler_params=_params(1),
        name="ffn",
    )(x, x_dec, w_in, w_out, ln_g, ln_b)


def kernel(x_prompt, x_sample, state_hgrn, ln_mix_g, ln_mix_b, ln_ffn_g, ln_ffn_b, a_lb_raw, a_w_in, a_norm_g, a_w_out, b_w_in, b_ln_g, b_ln_b, b_w_s, b_bias_s, b_w_out, ffn_w_in, ffn_w_out):
    bsz, seq, d = x_prompt.shape
    n_dec = x_sample.shape[0]
    assert x_sample.shape[1] == 1 and seq % CHUNK_B == 0

    masks = jnp.asarray(_level_masks(CHUNK_A))
    signs = jnp.asarray(_level_signs(CHUNK_A, DK_A))
    lb_raw = a_lb_raw.astype(F32)
    bias_full = jnp.repeat(jnp.swapaxes(b_bias_s, 1, 2), DG_B, axis=2)

    def rows(p):
        return p[:, None, :]

    mix_g, mix_b, ffn_g, ffn_b = rows(ln_mix_g), rows(ln_mix_b), rows(ln_ffn_g), rows(ln_ffn_b)
    norm_g, v_g, v_b = rows(a_norm_g), rows(b_ln_g), rows(b_ln_b)

    xp = x_prompt
    xs = x_sample.reshape(n_dec, d)
    st_p, v_p, v_s = [], [], []
    st_s = None
    for layer in range(DEPTH):
        j = layer // N_MIXERS
        if layer % N_MIXERS == 0:
            xp, s, fwi, fwo = _hgrn_prompt(xp, layer, lb_raw, a_w_in, norm_g, a_w_out, mix_g,
                                           mix_b, masks, signs, ffn_w_in, ffn_w_out)
            st_p.append(s)
            xs, st_s = _hgrn_decode(xs, layer, lb_raw, a_w_in, norm_g, a_w_out, mix_g, mix_b,
                                    state_hgrn, st_s)
        else:
            xp, v, xs, vd, fwi, fwo = _cmlp(xp, xs, j, layer, b_w_in, v_g, v_b, b_w_s, bias_full,
                                            b_w_out, mix_g, mix_b, ffn_w_in, ffn_w_out)
            v_p.append(v)
            v_s.append(vd[:, None, :])
        xp, xs = _ffn(xp.reshape(bsz * seq, d), xs, layer, fwi, fwo, ffn_g, ffn_b)
        xp = xp.reshape(bsz, seq, d)

    return (xp, xs.reshape(n_dec, 1, d), jnp.stack(st_p), st_s, jnp.stack(v_p), jnp.stack(v_s))
```
